```python
import math
import jax, jax.numpy as jnp
from jax import lax
import numpy as np

D_MODEL = 2048
BATCH = 4
SEQ = 2048
DEPTH = 1

N_HEADS_A = 8
HEAD_DIM_A = 128
D_A = N_HEADS_A * HEAD_DIM_A
MOBA_BLOCK = 256
MOBA_TOPK = 3
Q_CHUNK = 32
N_BUCKETS = 32
MAX_DISTANCE = 128
N_HEADS_B = 8
QK_DIM_B = 128
V_DIM_B = 128
D_B_QK = N_HEADS_B * QK_DIM_B
D_B = N_HEADS_B * V_DIM_B
RET_CHUNK = 128
ROT_BASE = 10000.0
D_PLE = 256
EPS = 1e-6
SPLIT_SIZES = (D_A, D_A, D_A, D_A, D_B_QK, D_B_QK, D_B, D_B, D_MODEL, D_MODEL)
D_IN = sum(SPLIT_SIZES)

kernel_name = 'hybrid_moba_retention_gated_block'


def rms_norm(x, g):
    xf = x.astype(jnp.float32)
    y = xf * lax.rsqrt(jnp.mean(xf * xf, axis=-1, keepdims=True) + EPS)
    return (y * g.astype(jnp.float32)).astype(x.dtype)


def t5_bucket(n):
    max_exact = N_BUCKETS // 2
    nf = jnp.maximum(n, 1).astype(jnp.float32)
    large = max_exact + (jnp.log(nf / max_exact) / math.log(MAX_DISTANCE / max_exact)
                         * (N_BUCKETS - max_exact)).astype(jnp.int32)
    large = jnp.minimum(large, N_BUCKETS - 1)
    return jnp.where(n < max_exact, n, large)


def moba_attention(q, k, v, rel_bias):
    B, S, H, dh = q.shape
    f32 = jnp.float32
    L = MOBA_BLOCK
    nb = -(-S // L)
    s_pad = nb * L
    pad = ((0, 0), (0, s_pad - S), (0, 0), (0, 0))
    qh = jnp.pad(q.astype(f32), pad).transpose(0, 2, 1, 3)
    kb = jnp.pad(k.astype(f32), pad).transpose(0, 2, 1, 3).reshape(B, H, nb, L, dh)
    vb = jnp.pad(v.astype(f32), pad).transpose(0, 2, 1, 3).reshape(B, H, nb, L, dh)

    k_mean = jnp.mean(kb, axis=3)
    gate = jnp.einsum('bhsd,bhnd->bhsn', qh, k_mean)
    q_blk = jnp.arange(s_pad) // L
    past = jnp.arange(nb)[None, :] < q_blk[:, None]
    gate = jnp.where(past[None, None], gate, -jnp.inf)
    n_sel = max(1, min(MOBA_TOPK, nb - 1))
    top_val, top_idx = lax.top_k(gate, n_sel)
    sel_valid = jnp.isfinite(top_val)

    scale = dh ** -0.5
    rel_bias_t = rel_bias.astype(f32).T
    bi = jnp.arange(B)[:, None, None, None]
    hi = jnp.arange(H)[None, :, None, None]
    hi5 = jnp.arange(H)[None, :, None, None, None]
    ar_q = jnp.arange(Q_CHUNK)
    ar_l = jnp.arange(L)

    def chunk(c):
        q0 = c * Q_CHUNK
        qc = lax.dynamic_slice_in_dim(qh, q0, Q_CHUNK, axis=2)
        idx = lax.dynamic_slice_in_dim(top_idx, q0, Q_CHUNK, axis=2)
        valid = lax.dynamic_slice_in_dim(sel_valid, q0, Q_CHUNK, axis=2)
        qpos = q0 + ar_q
        k_sel = kb[bi, hi, idx]
        v_sel = vb[bi, hi, idx]
        s_sel = jnp.einsum('bhqd,bhqnld->bhqnl', qc, k_sel) * scale
        kpos_sel = idx[..., None] * L + ar_l
        dist_sel = qpos[None, None, :, None, None] - kpos_sel
        s_sel = s_sel + rel_bias_t[hi5, t5_bucket(jnp.maximum(dist_sel, 0))]
        s_sel = jnp.where(valid[..., None], s_sel, -jnp.inf)
        blk = q0 // L
        k_own = lax.dynamic_index_in_dim(kb, blk, axis=2, keepdims=False)
        v_own = lax.dynamic_index_in_dim(vb, blk, axis=2, keepdims=False)
        s_own = jnp.einsum('bhqd,bhld->bhql', qc, k_own) * scale
        dist_own = qpos[:, None] - (blk * L + ar_l)[None, :]
        s_own = s_own + rel_bias_t[:, t5_bucket(jnp.maximum(dist_own, 0))][None]
        s_own = jnp.where((dist_own >= 0)[None, None], s_own, -jnp.inf)
        logits = jnp.concatenate([s_sel.reshape(B, H, Q_CHUNK, n_sel * L), s_own], axis=-1)
        probs = jax.nn.softmax(logits, axis=-1)
        p_sel = probs[..., :n_sel * L].reshape(B, H, Q_CHUNK, n_sel, L)
        p_own = probs[..., n_sel * L:]
        return (jnp.einsum('bhqnl,bhqnld->bhqd', p_sel, v_sel)
                + jnp.einsum('bhql,bhld->bhqd', p_own, v_own))

    out = lax.map(chunk, jnp.arange(s_pad // Q_CHUNK))
    out = out.transpose(1, 0, 3, 2, 4).reshape(B, s_pad, H, dh)
    return out[:, :S]


def rotate_pairs(t, cos, sin):
    t2 = t.reshape(t.shape[:-1] + (t.shape[-1] // 2, 2))
    a, b = t2[..., 0], t2[..., 1]
    return jnp.stack([a * cos - b * sin, b * cos + a * sin], axis=-1).reshape(t.shape)


def retention(q, k, v):
    B, S, H, dk = q.shape
    dv = v.shape[-1]
    f32 = jnp.float32
    C = RET_CHUNK
    N = S // C
    pos = jnp.arange(S, dtype=f32)
    theta = 1.0 / (ROT_BASE ** jnp.linspace(0.0, 1.0, dk // 2, dtype=f32))
    ang = pos[:, None] * theta[None, :]
    cos, sin = jnp.cos(ang)[:, None, :], jnp.sin(ang)[:, None, :]
    q = rotate_pairs(q.astype(f32), cos, sin)
    k = rotate_pairs(k.astype(f32), cos, sin) * (dk ** -0.5)
    log_gamma = jnp.log(1.0 - 2.0 ** (-5.0 - jnp.arange(H, dtype=f32)))

    qc = q.reshape(B, N, C, H, dk).transpose(0, 3, 1, 2, 4)
    kc = k.reshape(B, N, C, H, dk).transpose(0, 3, 1, 2, 4)
    vc = v.astype(f32).reshape(B, N, C, H, dv).transpose(0, 3, 1, 2, 4)

    i = jnp.arange(C)
    diff = i[:, None] - i[None, :]
    decay_mask = jnp.where(diff >= 0,
                           jnp.exp(jnp.maximum(diff, 0).astype(f32) * log_gamma[:, None, None]),
                           0.0)
    inner = jnp.einsum('bhncd,bhnmd->bhncm', qc, kc) * decay_mask[None, :, None]
    o_inner = jnp.einsum('bhncm,bhnme->bhnce', inner, vc)

    zeta = jnp.exp((C - 1 - i).astype(f32)[None, :] * log_gamma[:, None])
    kv = jnp.einsum('bhncd,bhnce->bhnde', kc * zeta[None, :, None, :, None], vc)
    chunk_decay = jnp.exp(C * log_gamma)[None, :, None, None]

    def step(R, kv_n):
        return R * chunk_decay + kv_n, R

    _, R_prev = lax.scan(step, jnp.zeros((B, H, dk, dv), f32), kv.transpose(2, 0, 1, 3, 4))
    xi = jnp.exp((i + 1).astype(f32)[None, :] * log_gamma[:, None])
    o_cross = jnp.einsum('bhncd,nbhde->bhnce', qc * xi[None, :, None, :, None], R_prev)
    o = (o_inner + o_cross).reshape(B, H, S, dv)
    o = o * lax.rsqrt(jnp.mean(o * o, axis=-1, keepdims=True) + EPS)
    return o.transpose(0, 2, 1, 3)


def setup_inputs(seed: int = 0) -> dict:
    key = jax.random.key(seed)
    ks = jax.random.split(key, 13)
    f32 = jnp.float32
    nrm = lambda k, shape, s: jax.random.normal(k, shape, f32) * s
    return {
        'x': nrm(ks[0], (BATCH, SEQ, D_MODEL), 1.0),
        'p': nrm(ks[1], (DEPTH, BATCH, SEQ, D_PLE), 1.0),
        'g_mix': 1.0 + nrm(ks[2], (DEPTH, D_MODEL), 0.02),
        'w_in': nrm(ks[3], (DEPTH, D_MODEL, D_IN), D_MODEL ** -0.5),
        'w_a': nrm(ks[4], (DEPTH, D_A, D_MODEL), D_A ** -0.5),
        'w_b': nrm(ks[5], (DEPTH, D_B, D_MODEL), D_B ** -0.5),
        'w_out': nrm(ks[6], (DEPTH, D_MODEL, D_MODEL), D_MODEL ** -0.5),
        'g_ple': 1.0 + nrm(ks[7], (DEPTH, D_MODEL), 0.02),
        'w_ple_gate': nrm(ks[8], (DEPTH, D_MODEL, D_MODEL), D_MODEL ** -0.5),
        'w_ple_proj': nrm(ks[9], (DEPTH, D_PLE, D_MODEL), D_PLE ** -0.5),
        'rel_bias': nrm(ks[10], (N_BUCKETS, N_HEADS_A), 0.5),
        'g_final': 1.0 + nrm(ks[11], (D_MODEL,), 0.02),
    }


def reference(x, p, g_mix, w_in, w_a, w_b, w_out, g_ple, w_ple_gate, w_ple_proj, rel_bias, g_final):
    B, S, _ = x.shape
    split_at = [int(s) for s in np.cumsum(SPLIT_SIZES)[:-1]]
    for i in range(DEPTH):
        h = rms_norm(x, g_mix[i])
        proj = h @ w_in[i]
        q_a, k_a, v_a, z_a, q_b, k_b, v_b, z_b, g_a, g_b = jnp.split(proj, split_at, axis=-1)
        y_a = moba_attention(q_a.reshape(B, S, N_HEADS_A, HEAD_DIM_A),
                             k_a.reshape(B, S, N_HEADS_A, HEAD_DIM_A),
                             v_a.reshape(B, S, N_HEADS_A, HEAD_DIM_A), rel_bias)
        y_a = y_a.reshape(B, S, D_A).astype(x.dtype) * jax.nn.silu(z_a)
        y_b = retention(q_b.reshape(B, S, N_HEADS_B, QK_DIM_B),
                        k_b.reshape(B, S, N_HEADS_B, QK_DIM_B),
                        v_b.reshape(B, S, N_HEADS_B, V_DIM_B))
        y_b = y_b.reshape(B, S, D_B).astype(x.dtype) * jax.nn.silu(z_b)
        merged = jax.nn.sigmoid(g_a) * (y_a @ w_a[i]) + jax.nn.sigmoid(g_b) * (y_b @ w_b[i])
        x = x + merged @ w_out[i]
        hp = rms_norm(x, g_ple[i])
        x = x + jax.nn.sigmoid(hp @ w_ple_gate[i]) * (p[i] @ w_ple_proj[i])
    return rms_norm(x, g_final)
```

```python
import functools
import math

import numpy as np
import jax
import jax.numpy as jnp
from jax import lax
from jax.experimental import pallas as pl
from jax.experimental.pallas import tpu as pltpu

D_MODEL = 2048
N_HEADS = 8
HEAD_DIM = 128
D_A = N_HEADS * HEAD_DIM
MOBA_BLOCK = 256
MOBA_TOPK = 3
N_BUCKETS = 32
MAX_DISTANCE = 128
RET_CHUNK = 128
ROT_BASE = 10000.0
D_PLE = 256
EPS = 1e-6
D_IN = 8 * D_A + 2 * D_MODEL

COL_QA, COL_KA, COL_VA, COL_ZA, COL_QB, COL_KB, COL_VB, COL_ZB = (i * N_HEADS for i in range(8))
COL_GA = 8 * D_A // D_MODEL
COL_GB = COL_GA + 1

VMEM_LIMIT_V7X = 56 * 1024 * 1024

F32 = jnp.float32
BF16 = jnp.bfloat16
NEG_INF = float("-inf")
NT_DIMS = (((1,), (1,)), ((), ()))
TN_DIMS = (((0,), (0,)), ((), ()))


def _rms_norm(x, g):
    return x * lax.rsqrt(jnp.mean(x * x, axis=-1, keepdims=True) + EPS) * g


def _proj_kernel(x_ref, g_ref, w_ref, o_ref, h_ref):
    @pl.when(pl.program_id(1) == 0)
    def _():
        h_ref[...] = _rms_norm(x_ref[...], g_ref[...]).astype(BF16)

    o_ref[...] = jnp.dot(h_ref[...], w_ref[...], preferred_element_type=F32).astype(o_ref.dtype)


def _project(x2d, g, w_bf16, tm, tn):
    m, d = x2d.shape
    n = w_bf16.shape[1]
    return pl.pallas_call(
        _proj_kernel,
        name="proj",
        grid=(m // tm, n // tn),
        in_specs=[
            pl.BlockSpec((tm, d), lambda i, j: (i, 0)),
            pl.BlockSpec((1, d), lambda i, j: (0, 0)),
            pl.BlockSpec((d, tn), lambda i, j: (0, j)),
        ],
        out_specs=pl.BlockSpec((tm, tn), lambda i, j: (i, j)),
        out_shape=jax.ShapeDtypeStruct((m, n), BF16),
        scratch_shapes=[pltpu.VMEM((tm, d), BF16)],
        compiler_params=pltpu.CompilerParams(
            dimension_semantics=("parallel", "arbitrary"),
            vmem_limit_bytes=VMEM_LIMIT_V7X),
    )(x2d, g, w_bf16)


def _t5_bucket_np(n):
    max_exact = N_BUCKETS // 2
    nf = np.maximum(n, 1).astype(np.float32)
    val = (np.log(nf / max_exact) / np.float32(math.log(MAX_DISTANCE / max_exact))
           * (N_BUCKETS - max_exact))
    large = np.minimum(max_exact + val.astype(np.int32), N_BUCKETS - 1)
    return np.where(n < max_exact, n, large).astype(np.int32)


def _bucket_tables():
    i = np.arange(MOBA_BLOCK)[:, None]
    j = np.arange(MOBA_BLOCK)[None, :]
    own = _t5_bucket_np(np.maximum(i - j, 0))
    own = np.where(j > i, -1, own)
    prev = _t5_bucket_np(MOBA_BLOCK + i - j)
    return np.stack([own, prev]).astype(np.int32)


def _bias_kernel(rb_ref, bkt_ref, o_ref):
    h = pl.program_id(0)
    bkt = bkt_ref[...]
    tab = jnp.where(bkt < 0, NEG_INF, 0.0).astype(F32)
    for b in range(N_BUCKETS):
        tab = jnp.where(bkt == b, rb_ref[b, h], tab)
    o_ref[...] = tab


def _bias_tables(rel_bias):
    bkt = jnp.asarray(_bucket_tables())
    L = MOBA_BLOCK
    return pl.pallas_call(
        _bias_kernel,
        name="t5_bias",
        grid=(N_HEADS,),
        in_specs=[
            pl.BlockSpec(memory_space=pltpu.SMEM),
            pl.BlockSpec((2, L, L), lambda h: (0, 0, 0)),
        ],
        out_specs=pl.BlockSpec((None, 2, L, L), lambda h: (h, 0, 0, 0)),
        out_shape=jax.ShapeDtypeStruct((N_HEADS, 2, L, L), F32),
    )(rel_bias, bkt)


def _moba_kernel(rb_ref, q_ref, k_ref, v_ref, z_ref, tab_ref, o_ref,
                 kmean_ref, m_ref, l_ref, acc_ref):
    L = MOBA_BLOCK
    nb = k_ref.shape[0] // L
    h = pl.program_id(1)
    qi = pl.program_id(2)
    scale = HEAD_DIM ** -0.5

    @pl.when(qi == 0)
    def _():
        kf = k_ref[...].astype(F32).reshape(nb, L, HEAD_DIM)
        kmean_ref[...] = jnp.mean(kf, axis=1)

    q = q_ref[...]

    km = kmean_ref[...]
    km_hi = km.astype(BF16)
    km_lo = (km - km_hi.astype(F32)).astype(BF16)
    gate = (lax.dot_general(q, km_hi, NT_DIMS, preferred_element_type=F32)
            + lax.dot_general(q, km_lo, NT_DIMS, preferred_element_type=F32))
    col = lax.broadcasted_iota(jnp.int32, (L, nb), 1)
    cand = col < qi
    g = jnp.where(cand, gate, NEG_INF)
    cnt = jnp.zeros((L, nb), jnp.int32)
    for mm in range(nb):
        gm = g[:, mm:mm + 1]
        beats = jnp.where(gm > g, 1, jnp.where(gm == g, jnp.where(mm < col, 1, 0), 0))
        cnt = cnt + beats
    sel_add = jnp.where(cand, jnp.where(cnt < MOBA_TOPK, 0.0, NEG_INF), NEG_INF)

    def block_update(kb, vb, s_bias):
        s = lax.dot_general(q, kb, NT_DIMS, preferred_element_type=F32) * scale + s_bias
        m_old = m_ref[...]
        m_new = jnp.maximum(m_old, jnp.max(s, axis=-1, keepdims=True))
        alpha = jnp.exp(m_old - m_new)
        p = jnp.exp(s - m_new)
        l_ref[...] = alpha * l_ref[...] + jnp.sum(p, axis=-1, keepdims=True)
        acc_ref[...] = alpha * acc_ref[...] + jnp.dot(p.astype(BF16), vb,
                                                      preferred_element_type=F32)
        m_ref[...] = m_new

    own_off = pl.multiple_of(qi * L, L)
    s0 = lax.dot_general(q, k_ref[pl.ds(own_off, L), :], NT_DIMS,
                         preferred_element_type=F32) * scale + tab_ref[0]
    m0 = jnp.max(s0, axis=-1, keepdims=True)
    p0 = jnp.exp(s0 - m0)
    m_ref[...] = m0
    l_ref[...] = jnp.sum(p0, axis=-1, keepdims=True)
    acc_ref[...] = jnp.dot(p0.astype(BF16), v_ref[pl.ds(own_off, L), :],
                           preferred_element_type=F32)

    def sel_column(n):
        return jnp.max(jnp.where(col == n, sel_add, NEG_INF), axis=-1, keepdims=True)

    @pl.when(qi >= 1)
    def _():
        n = qi - 1
        off = pl.multiple_of(n * L, L)
        block_update(k_ref[pl.ds(off, L), :], v_ref[pl.ds(off, L), :],
                     tab_ref[1] + sel_column(n))

    far_bias = rb_ref[N_BUCKETS - 1, h]

    def far_body(n, carry):
        off = pl.multiple_of(n * L, L)
        block_update(k_ref[pl.ds(off, L), :], v_ref[pl.ds(off, L), :],
                     far_bias + sel_column(n))
        return carry

    lax.fori_loop(0, jnp.maximum(qi - 1, 0), far_body, 0)

    z = z_ref[...].astype(F32)
    y = acc_ref[...] / l_ref[...]
    o_ref[...] = (y * (z * jax.nn.sigmoid(z))).astype(o_ref.dtype)


def _moba(proj3, tabs, rel_bias):
    B, S, _ = proj3.shape
    L = MOBA_BLOCK
    nq = S // L
    return pl.pallas_call(
        _moba_kernel,
        name="moba",
        grid=(B, N_HEADS, nq),
        in_specs=[
            pl.BlockSpec(memory_space=pltpu.SMEM),
            pl.BlockSpec((None, L, HEAD_DIM), lambda b, h, i: (b, i, COL_QA + h)),
            pl.BlockSpec((None, S, HEAD_DIM), lambda b, h, i: (b, 0, COL_KA + h)),
            pl.BlockSpec((None, S, HEAD_DIM), lambda b, h, i: (b, 0, COL_VA + h)),
            pl.BlockSpec((None, L, HEAD_DIM), lambda b, h, i: (b, i, COL_ZA + h)),
            pl.BlockSpec((None, 2, L, L), lambda b, h, i: (h, 0, 0, 0)),
        ],
        out_specs=pl.BlockSpec((None, L, HEAD_DIM), lambda b, h, i: (b, i, h)),
        out_shape=jax.ShapeDtypeStruct((B, S, D_A), BF16),
        scratch_shapes=[
            pltpu.VMEM((S // L, HEAD_DIM), F32),
            pltpu.VMEM((L, 1), F32),
            pltpu.VMEM((L, 1), F32),
            pltpu.VMEM((L, HEAD_DIM), F32),
        ],
        compiler_params=pltpu.CompilerParams(
            dimension_semantics=("parallel", "parallel", "arbitrary")),
    )(rel_bias, proj3, proj3, proj3, proj3, tabs)


def _retention_tables(S):
    C = RET_CHUNK
    dk = HEAD_DIM
    pos = jnp.arange(S, dtype=F32)
    theta = 1.0 / (ROT_BASE ** jnp.linspace(0.0, 1.0, dk // 2, dtype=F32))
    ang = pos[:, None] * theta[None, :]
    cos = jnp.repeat(jnp.cos(ang), 2, axis=1)
    sin = jnp.repeat(jnp.sin(ang), 2, axis=1) * jnp.tile(jnp.array([-1.0, 1.0], F32), dk // 2)
    log_gamma = jnp.log(1.0 - 2.0 ** (-5.0 - jnp.arange(N_HEADS, dtype=F32)))
    i = jnp.arange(C)
    diff = i[:, None] - i[None, :]
    k_scale = dk ** -0.5
    decay = jnp.where(diff >= 0,
                      jnp.exp(jnp.maximum(diff, 0).astype(F32) * log_gamma[:, None, None]),
                      0.0) * k_scale
    zeta = jnp.exp((C - 1 - i).astype(F32)[None, :] * log_gamma[:, None]) * k_scale
    xi = jnp.exp((i + 1).astype(F32)[None, :] * log_gamma[:, None])
    zeta = jnp.broadcast_to(zeta[:, :, None], (N_HEADS, C, dk))
    xi = jnp.broadcast_to(xi[:, :, None], (N_HEADS, C, dk))
    chunk_decay = jnp.exp(C * log_gamma)
    return cos, sin, decay, zeta, xi, chunk_decay


def _ret_kernel(cd_ref, q_ref, k_ref, v_ref, z_ref, cos_ref, sin_ref, dm_ref, zeta_ref, xi_ref,
                o_ref, r_ref):
    h = pl.program_id(1)

    @pl.when(pl.program_id(2) == 0)
    def _():
        r_ref[...] = jnp.zeros_like(r_ref)

    cosv = cos_ref[...]
    sinv = sin_ref[...]
    lane = lax.broadcasted_iota(jnp.int32, cosv.shape, 1)
    even = (lane % 2) == 0

    def rot(t):
        partner = jnp.where(even, pltpu.roll(t, HEAD_DIM - 1, 1), pltpu.roll(t, 1, 1))
        return t * cosv + partner * sinv

    q = rot(q_ref[...].astype(F32))
    k = rot(k_ref[...].astype(F32))
    v = v_ref[...]
    qb = q.astype(BF16)
    inner = lax.dot_general(qb, k.astype(BF16), NT_DIMS, preferred_element_type=F32) * dm_ref[...]
    r_old = r_ref[...]
    o = (jnp.dot(inner.astype(BF16), v, preferred_element_type=F32)
         + xi_ref[...] * jnp.dot(qb, r_old.astype(BF16), preferred_element_type=F32))
    kz = (k * zeta_ref[...]).astype(BF16)
    r_ref[...] = r_old * cd_ref[h] + lax.dot_general(kz, v, TN_DIMS, preferred_element_type=F32)
    o = o * lax.rsqrt(jnp.mean(o * o, axis=-1, keepdims=True) + EPS)
    z = z_ref[...].astype(F32)
    o_ref[...] = (o * (z * jax.nn.sigmoid(z))).astype(o_ref.dtype)


def _retention(proj3):
    B, S, _ = proj3.shape
    C = RET_CHUNK
    dk = HEAD_DIM
    cos, sin, decay, zeta, xi, chunk_decay = _retention_tables(S)
    head_tab = pl.BlockSpec((None, C, dk), lambda b, h, c: (h, 0, 0))
    return pl.pallas_call(
        _ret_kernel,
        name="retention",
        grid=(B, N_HEADS, S // C),
        in_specs=[
            pl.BlockSpec(memory_space=pltpu.SMEM),
            pl.BlockSpec((None, C, dk), lambda b, h, c: (b, c, COL_QB + h)),
            pl.BlockSpec((None, C, dk), lambda b, h, c: (b, c, COL_KB + h)),
            pl.BlockSpec((None, C, dk), lambda b, h, c: (b, c, COL_VB + h)),
            pl.BlockSpec((None, C, dk), lambda b, h, c: (b, c, COL_ZB + h)),
            pl.BlockSpec((C, dk), lambda b, h, c: (c, 0)),
            pl.BlockSpec((C, dk), lambda b, h, c: (c, 0)),
            head_tab, head_tab, head_tab,
        ],
        out_specs=pl.BlockSpec((None, C, dk), lambda b, h, c: (b, c, h)),
        out_shape=jax.ShapeDtypeStruct((B, S, N_HEADS * dk), BF16),
        scratch_shapes=[pltpu.VMEM((dk, dk), F32)],
        compiler_params=pltpu.CompilerParams(
            dimension_semantics=("parallel", "parallel", "arbitrary")),
    )(chunk_decay, proj3, proj3, proj3, proj3, cos, sin, decay, zeta, xi)


def _tail_kernel(ya_ref, yb_ref, ga_ref, gb_ref, x_ref, p_ref, wa_ref, wb_ref, wo_ref, wpg_ref,
                 wpp_ref, gple_ref, gfin_ref, o_ref):
    ua = jnp.dot(ya_ref[...], wa_ref[...], preferred_element_type=F32)
    ub = jnp.dot(yb_ref[...], wb_ref[...], preferred_element_type=F32)
    merged = (jax.nn.sigmoid(ga_ref[...].astype(F32)) * ua
              + jax.nn.sigmoid(gb_ref[...].astype(F32)) * ub)
    x1 = x_ref[...] + jnp.dot(merged.astype(BF16), wo_ref[...], preferred_element_type=F32)
    hp = _rms_norm(x1, gple_ref[...]).astype(BF16)
    gate = jax.nn.sigmoid(jnp.dot(hp, wpg_ref[...], preferred_element_type=F32))
    pe = jnp.dot(p_ref[...].astype(BF16), wpp_ref[...], preferred_element_type=F32)
    x2 = x1 + gate * pe
    o_ref[...] = _rms_norm(x2, gfin_ref[...])


def _tail(ya, yb, proj, x2d, p2d, wa, wb, wo, wpg, wpp, g_ple, g_final, tm):
    m, d = x2d.shape

    def const(shape):
        return pl.BlockSpec(shape, lambda i: (0, 0), pipeline_mode=pl.Buffered(1))

    return pl.pallas_call(
        _tail_kernel,
        name="tail",
        grid=(m // tm,),
        in_specs=[
            pl.BlockSpec((tm, D_A), lambda i: (i, 0)),
            pl.BlockSpec((tm, D_A), lambda i: (i, 0)),
            pl.BlockSpec((tm, d), lambda i: (i, COL_GA)),
            pl.BlockSpec((tm, d), lambda i: (i, COL_GB)),
            pl.BlockSpec((tm, d), lambda i: (i, 0)),
            pl.BlockSpec((tm, D_PLE), lambda i: (i, 0)),
            const((D_A, d)), const((D_A, d)), const((d, d)), const((d, d)), const((D_PLE, d)),
            const((1, d)), const((1, d)),
        ],
        out_specs=pl.BlockSpec((tm, d), lambda i: (i, 0)),
        out_shape=jax.ShapeDtypeStruct((m, d), F32),
        compiler_params=pltpu.CompilerParams(
            dimension_semantics=("parallel",),
            vmem_limit_bytes=VMEM_LIMIT_V7X),
    )(ya, yb, proj, proj, x2d, p2d, wa, wb, wo, wpg, wpp, g_ple, g_final)


def kernel(x, p, g_mix, w_in, w_a, w_b, w_out, g_ple, w_ple_gate, w_ple_proj, rel_bias, g_final):
    B, S, d = x.shape
    assert w_in.shape[0] == 1, "the fused tail applies the final norm: single-layer stacks only"
    tabs = _bias_tables(rel_bias)
    x2d = x.reshape(B * S, d)
    proj = _project(x2d, g_mix, w_in[0].astype(BF16), tm=1024, tn=1024)
    proj3 = proj.reshape(B, S, D_IN)
    ya = _moba(proj3, tabs, rel_bias).reshape(B * S, D_A)
    yb = _retention(proj3).reshape(B * S, D_A)
    out = _tail(ya, yb, proj, x2d, p[0].reshape(B * S, D_PLE),
                w_a[0].astype(BF16), w_b[0].astype(BF16), w_out[0].astype(BF16),
                w_ple_gate[0].astype(BF16), w_ple_proj[0].astype(BF16),
                g_ple, g_final[None, :], tm=256)
    return out.reshape(B, S, d)
```

```python
import math

import numpy as np
import jax
import jax.numpy as jnp
from jax import lax
from jax.experimental import pallas as pl
from jax.experimental.pallas import tpu as pltpu

D_MODEL = 2048
N_HEADS = 8
HEAD_DIM = 128
D_A = N_HEADS * HEAD_DIM
MOBA_BLOCK = 256
MOBA_TOPK = 3
N_BUCKETS = 32
MAX_DISTANCE = 128
RET_CHUNK = 128
ROT_BASE = 10000.0
D_PLE = 256
EPS = 1e-6
D_IN = 8 * D_A + 2 * D_MODEL

COL_QA, COL_KA, COL_VA, COL_ZA = (i * N_HEADS for i in range(4))
COL_QB, COL_KB, COL_VB, COL_ZB = 4, 5, 6, 7
COL_GA = 8 * D_A // D_MODEL
COL_GB = COL_GA + 1

VMEM_LIMIT_V7X = 56 * 1024 * 1024

F32 = jnp.float32
BF16 = jnp.bfloat16
NEG_INF = float("-inf")
NT_DIMS = (((1,), (1,)), ((), ()))
TN_DIMS = (((0,), (0,)), ((), ()))


def _rms_norm(x, g):
    return x * lax.rsqrt(jnp.mean(x * x, axis=-1, keepdims=True) + EPS) * g


def _silu(z):
    return z * jax.nn.sigmoid(z)


def _proj_kernel(x_ref, g_ref, w_ref, o_ref, h_ref):
    @pl.when(pl.program_id(1) == 0)
    def _():
        h_ref[...] = _rms_norm(x_ref[...], g_ref[...]).astype(BF16)

    o_ref[...] = jnp.dot(h_ref[...], w_ref[...], preferred_element_type=F32).astype(o_ref.dtype)


def _project(x2d, g, w_bf16, tm, tn):
    m, d = x2d.shape
    n = w_bf16.shape[1]
    return pl.pallas_call(
        _proj_kernel,
        name="proj",
        grid=(m // tm, n // tn),
        in_specs=[
            pl.BlockSpec((tm, d), lambda i, j: (i, 0)),
            pl.BlockSpec((1, d), lambda i, j: (0, 0)),
            pl.BlockSpec((d, tn), lambda i, j: (0, j)),
        ],
        out_specs=pl.BlockSpec((tm, tn), lambda i, j: (i, j)),
        out_shape=jax.ShapeDtypeStruct((m, n), BF16),
        scratch_shapes=[pltpu.VMEM((tm, d), BF16)],
        compiler_params=pltpu.CompilerParams(
            dimension_semantics=("parallel", "arbitrary"),
            vmem_limit_bytes=VMEM_LIMIT_V7X),
    )(x2d, g, w_bf16)


def _t5_bucket_np(n):
    max_exact = N_BUCKETS // 2
    nf = np.maximum(n, 1).astype(np.float32)
    val = (np.log(nf / max_exact) / np.float32(math.log(MAX_DISTANCE / max_exact))
           * (N_BUCKETS - max_exact))
    large = np.minimum(max_exact + val.astype(np.int32), N_BUCKETS - 1)
    return np.where(n < max_exact, n, large).astype(np.int32)


def _bucket_tables():
    j = np.arange(MOBA_BLOCK)[:, None]
    i = np.arange(MOBA_BLOCK)[None, :]
    own = _t5_bucket_np(np.maximum(i - j, 0))
    own = np.where(j > i, -1, own)
    prev = _t5_bucket_np(MOBA_BLOCK + i - j)
    return np.stack([own, prev]).astype(np.int32)


def _bias_kernel(rb_ref, bkt_ref, o_ref):
    h = pl.program_id(0)
    bkt = bkt_ref[...]
    tab = jnp.where(bkt < 0, NEG_INF, 0.0).astype(F32)
    for b in range(N_BUCKETS):
        tab = jnp.where(bkt == b, rb_ref[b, h], tab)
    o_ref[...] = tab


def _bias_tables(rel_bias):
    bkt = jnp.asarray(_bucket_tables())
    L = MOBA_BLOCK
    return pl.pallas_call(
        _bias_kernel,
        name="t5_bias",
        grid=(N_HEADS,),
        in_specs=[
            pl.BlockSpec(memory_space=pltpu.SMEM),
            pl.BlockSpec((2, L, L), lambda h: (0, 0, 0)),
        ],
        out_specs=pl.BlockSpec((None, 2, L, L), lambda h: (h, 0, 0, 0)),
        out_shape=jax.ShapeDtypeStruct((N_HEADS, 2, L, L), F32),
    )(rel_bias, bkt)


def _moba_kernel(rb_ref, q_ref, k_ref, v_ref, z_ref, tab_ref, o_ref, vt_ref, s_ref):
    L = MOBA_BLOCK
    nb = k_ref.shape[0] // L
    h = pl.program_id(1)

    kf = k_ref[...].astype(F32).reshape(nb, L, HEAD_DIM)
    km = jnp.mean(kf, axis=1)
    km_hi = km.astype(BF16)
    km_lo = (km - km_hi.astype(F32)).astype(BF16)
    km2 = jnp.concatenate([km_hi, km_lo], axis=0)
    vt_ref[...] = v_ref[...].T
    far_bias = rb_ref[N_BUCKETS - 1, h]
    row = lax.broadcasted_iota(jnp.int32, (nb, L), 0)

    for qi in range(nb):
        q = q_ref[qi * L:(qi + 1) * L, :]
        sel_add = None
        if qi > MOBA_TOPK:
            g2 = lax.dot_general(km2, q, NT_DIMS, preferred_element_type=F32)
            g = g2[:nb] + g2[nb:]
            cnt = jnp.zeros((nb, L), jnp.int32)
            for mm in range(qi):
                gm = g[mm:mm + 1, :]
                ge = jnp.where(gm >= g, 1, 0)
                gt = jnp.where(gm > g, 1, 0)
                cnt = cnt + jnp.where(row > mm, ge, gt)
            sel_add = jnp.where(cnt < MOBA_TOPK, 0.0, NEG_INF)

        sbuf = s_ref.at[qi % 2]
        m = None
        for n in range(qi + 1):
            s = lax.dot_general(k_ref[n * L:(n + 1) * L, :], q, NT_DIMS,
                                preferred_element_type=F32)
            if n == qi:
                s = s + tab_ref[0]
            elif n == qi - 1:
                s = s + tab_ref[1]
                if sel_add is not None:
                    s = s + sel_add[n:n + 1, :]
            elif sel_add is not None:
                s = s + (sel_add[n:n + 1, :] + far_bias)
            else:
                s = s + far_bias
            sbuf[n * L:(n + 1) * L, :] = s
            tmax = jnp.max(s, axis=0, keepdims=True)
            m = tmax if m is None else jnp.maximum(m, tmax)

        l = None
        acc = None
        for n in range(qi + 1):
            p = jnp.exp(sbuf[n * L:(n + 1) * L, :] - m)
            psum = jnp.sum(p, axis=0, keepdims=True)
            pv = jnp.dot(vt_ref[:, n * L:(n + 1) * L], p.astype(BF16),
                         preferred_element_type=F32)
            l = psum if l is None else l + psum
            acc = pv if acc is None else acc + pv
        y = (acc * (1.0 / l)).T
        z = z_ref[qi * L:(qi + 1) * L, :].astype(F32)
        o_ref[qi * L:(qi + 1) * L, :] = (y * _silu(z)).astype(o_ref.dtype)


def _moba(proj3, tabs, rel_bias):
    B, S, _ = proj3.shape
    L = MOBA_BLOCK

    def head_cols(col0):
        return pl.BlockSpec((None, S, HEAD_DIM), lambda b, h: (b, 0, col0 + h))

    return pl.pallas_call(
        _moba_kernel,
        name="moba",
        grid=(B, N_HEADS),
        in_specs=[
            pl.BlockSpec(memory_space=pltpu.SMEM),
            head_cols(COL_QA), head_cols(COL_KA), head_cols(COL_VA), head_cols(COL_ZA),
            pl.BlockSpec((None, 2, L, L), lambda b, h: (h, 0, 0, 0)),
        ],
        out_specs=head_cols(0),
        out_shape=jax.ShapeDtypeStruct((B, S, D_A), BF16),
        scratch_shapes=[
            pltpu.VMEM((HEAD_DIM, S), BF16),
            pltpu.VMEM((2, S, L), F32),
        ],
        compiler_params=pltpu.CompilerParams(
            dimension_semantics=("parallel", "parallel"),
            vmem_limit_bytes=VMEM_LIMIT_V7X),
    )(rel_bias, proj3, proj3, proj3, proj3, tabs)


def _retention_tables(S):
    C = RET_CHUNK
    dk = HEAD_DIM
    pos = jnp.arange(S, dtype=F32)
    theta = 1.0 / (ROT_BASE ** jnp.linspace(0.0, 1.0, dk // 2, dtype=F32))
    ang = pos[:, None] * theta[None, :]
    cos = jnp.repeat(jnp.cos(ang), 2, axis=1)
    sin = jnp.repeat(jnp.sin(ang), 2, axis=1) * jnp.tile(jnp.array([-1.0, 1.0], F32), dk // 2)
    log_gamma = jnp.log(1.0 - 2.0 ** (-5.0 - jnp.arange(N_HEADS, dtype=F32)))
    i = jnp.arange(C)
    diff = i[:, None] - i[None, :]
    k_scale = dk ** -0.5
    decay = jnp.where(diff >= 0,
                      jnp.exp(jnp.maximum(diff, 0).astype(F32) * log_gamma[:, None, None]),
                      0.0) * k_scale
    zeta = jnp.exp((C - 1 - i).astype(F32)[None, :] * log_gamma[:, None]) * k_scale
    xi = jnp.exp((i + 1).astype(F32)[None, :] * log_gamma[:, None])
    zeta = jnp.broadcast_to(zeta[:, :, None], (N_HEADS, C, dk))
    xi = jnp.broadcast_to(xi[:, :, None], (N_HEADS, C, dk))
    chunk_decay = jnp.exp(C * log_gamma)
    return cos, sin, decay, zeta, xi, chunk_decay


def _ret_kernel(cd_ref, q_ref, k_ref, v_ref, z_ref, cos_ref, sin_ref, dm_ref, zeta_ref, xi_ref,
                o_ref, r_ref):
    @pl.when(pl.program_id(1) == 0)
    def _():
        r_ref[...] = jnp.zeros_like(r_ref)

    cosv = cos_ref[...]
    sinv = sin_ref[...]
    lane = lax.broadcasted_iota(jnp.int32, cosv.shape, 1)
    even = (lane % 2) == 0

    def rot(t):
        partner = jnp.where(even, pltpu.roll(t, HEAD_DIM - 1, 1), pltpu.roll(t, 1, 1))
        return t * cosv + partner * sinv

    for h in range(N_HEADS):
        cols = slice(h * HEAD_DIM, (h + 1) * HEAD_DIM)
        q = rot(q_ref[:, cols].astype(F32))
        k = rot(k_ref[:, cols].astype(F32))
        v = v_ref[:, cols]
        qb = q.astype(BF16)
        inner = lax.dot_general(qb, k.astype(BF16), NT_DIMS,
                                preferred_element_type=F32) * dm_ref[h]
        r_old = r_ref[h]
        o = (jnp.dot(inner.astype(BF16), v, preferred_element_type=F32)
             + xi_ref[h] * jnp.dot(qb, r_old.astype(BF16), preferred_element_type=F32))
        kz = (k * zeta_ref[h]).astype(BF16)
        r_ref[h] = r_old * cd_ref[h] + lax.dot_general(kz, v, TN_DIMS,
                                                       preferred_element_type=F32)
        o = o * lax.rsqrt(jnp.mean(o * o, axis=-1, keepdims=True) + EPS)
        o_ref[:, cols] = (o * _silu(z_ref[:, cols].astype(F32))).astype(o_ref.dtype)


def _retention(proj3):
    B, S, _ = proj3.shape
    C = RET_CHUNK
    dk = HEAD_DIM
    cos, sin, decay, zeta, xi, chunk_decay = _retention_tables(S)

    def all_heads(col0):
        return pl.BlockSpec((None, C, D_A), lambda b, c: (b, c, col0))

    head_tab = pl.BlockSpec((N_HEADS, C, dk), lambda b, c: (0, 0, 0))
    return pl.pallas_call(
        _ret_kernel,
        name="retention",
        grid=(B, S // C),
        in_specs=[
            pl.BlockSpec(memory_space=pltpu.SMEM),
            all_heads(COL_QB), all_heads(COL_KB), all_heads(COL_VB), all_heads(COL_ZB),
            pl.BlockSpec((C, dk), lambda b, c: (c, 0)),
            pl.BlockSpec((C, dk), lambda b, c: (c, 0)),
            head_tab, head_tab, head_tab,
        ],
        out_specs=all_heads(0),
        out_shape=jax.ShapeDtypeStruct((B, S, D_A), BF16),
        scratch_shapes=[pltpu.VMEM((N_HEADS, dk, dk), F32)],
        compiler_params=pltpu.CompilerParams(
            dimension_semantics=("parallel", "arbitrary")),
    )(chunk_decay, proj3, proj3, proj3, proj3, cos, sin, decay, zeta, xi)


def _tail_kernel(ya_ref, yb_ref, ga_ref, gb_ref, x_ref, p_ref, wa_ref, wb_ref, wo_ref, wpg_ref,
                 wpp_ref, gple_ref, gfin_ref, o_ref):
    ua = jnp.dot(ya_ref[...], wa_ref[...], preferred_element_type=F32)
    ub = jnp.dot(yb_ref[...], wb_ref[...], preferred_element_type=F32)
    merged = (jax.nn.sigmoid(ga_ref[...].astype(F32)) * ua
              + jax.nn.sigmoid(gb_ref[...].astype(F32)) * ub)
    x1 = x_ref[...] + jnp.dot(merged.astype(BF16), wo_ref[...], preferred_element_type=F32)
    hp = _rms_norm(x1, gple_ref[...]).astype(BF16)
    gate = jax.nn.sigmoid(jnp.dot(hp, wpg_ref[...], preferred_element_type=F32))
    pe = jnp.dot(p_ref[...].astype(BF16), wpp_ref[...], preferred_element_type=F32)
    x2 = x1 + gate * pe
    o_ref[...] = _rms_norm(x2, gfin_ref[...])


def _tail(ya, yb, proj, x2d, p2d, wa, wb, wo, wpg, wpp, g_ple, g_final, tm):
    m, d = x2d.shape

    def const(shape):
        return pl.BlockSpec(shape, lambda i: (0, 0), pipeline_mode=pl.Buffered(1))

    return pl.pallas_call(
        _tail_kernel,
        name="tail",
        grid=(m // tm,),
        in_specs=[
            pl.BlockSpec((tm, D_A), lambda i: (i, 0)),
            pl.BlockSpec((tm, D_A), lambda i: (i, 0)),
            pl.BlockSpec((tm, d), lambda i: (i, COL_GA)),
            pl.BlockSpec((tm, d), lambda i: (i, COL_GB)),
            pl.BlockSpec((tm, d), lambda i: (i, 0)),
            pl.BlockSpec((tm, D_PLE), lambda i: (i, 0)),
            const((D_A, d)), const((D_A, d)), const((d, d)), const((d, d)), const((D_PLE, d)),
            const((1, d)), const((1, d)),
        ],
        out_specs=pl.BlockSpec((tm, d), lambda i: (i, 0)),
        out_shape=jax.ShapeDtypeStruct((m, d), F32),
        compiler_params=pltpu.CompilerParams(
            dimension_semantics=("parallel",),
            vmem_limit_bytes=VMEM_LIMIT_V7X),
    )(ya, yb, proj, proj, x2d, p2d, wa, wb, wo, wpg, wpp, g_ple, g_final)


def kernel(x, p, g_mix, w_in, w_a, w_b, w_out, g_ple, w_ple_gate, w_ple_proj, rel_bias, g_final):
    B, S, d = x.shape
    assert w_in.shape[0] == 1, "the fused tail applies the final norm: single-layer stacks only"
    tabs = _bias_tables(rel_bias)
    x2d = x.reshape(B * S, d)
    col_scale = jnp.where(jnp.arange(D_IN) < D_A, HEAD_DIM ** -0.5, 1.0).astype(F32)
    w_in_bf16 = (w_in[0] * col_scale[None, :]).astype(BF16)
    proj = _project(x2d, g_mix, w_in_bf16, tm=1024, tn=1024)
    proj3 = proj.reshape(B, S, D_IN)
    ya = _moba(proj3, tabs, rel_bias).reshape(B * S, D_A)
    yb = _retention(proj3).reshape(B * S, D_A)
    out = _tail(ya, yb, proj, x2d, p[0].reshape(B * S, D_PLE),
                w_a[0].astype(BF16), w_b[0].astype(BF16), w_out[0].astype(BF16),
                w_ple_gate[0].astype(BF16), w_ple_proj[0].astype(BF16),
                g_ple, g_final[None, :], tm=256)
    return out.reshape(B, S, d)
```

```python
import math

import numpy as np
import jax
import jax.numpy as jnp
from jax import lax
from jax.experimental import pallas as pl
from jax.experimental.pallas import tpu as pltpu

D_MODEL = 2048
N_HEADS = 8
HEAD_DIM = 128
D_A = N_HEADS * HEAD_DIM
MOBA_BLOCK = 256
MOBA_TOPK = 3
N_BUCKETS = 32
MAX_DISTANCE = 128
RET_CHUNK = 128
ROT_BASE = 10000.0
D_PLE = 256
EPS = 1e-6
D_IN = 8 * D_A + 2 * D_MODEL

COL_QA, COL_KA, COL_VA, COL_ZA = (i * N_HEADS for i in range(4))
COL_QB, COL_KB, COL_VB, COL_ZB = 4, 5, 6, 7
COL_GA = 8 * D_A // D_MODEL
COL_GB = COL_GA + 1

VMEM_LIMIT_V7X = 56 * 1024 * 1024
BF16_SUBLANES = 16
VT_PAD = BF16_SUBLANES

F32 = jnp.float32
BF16 = jnp.bfloat16
NEG_INF = float("-inf")
LOG2E = math.log2(math.e)
NT_DIMS = (((1,), (1,)), ((), ()))
TN_DIMS = (((0,), (0,)), ((), ()))


def _rms_norm(x, g):
    return x * lax.rsqrt(jnp.mean(x * x, axis=-1, keepdims=True) + EPS) * g


def _silu(z):
    return z * jax.nn.sigmoid(z)


def _proj_kernel(x_ref, g_ref, w_ref, cs_ref, o_ref, h_ref):
    @pl.when(pl.program_id(1) == 0)
    def _():
        h_ref[...] = _rms_norm(x_ref[...], g_ref[...]).astype(BF16)

    acc = jnp.dot(h_ref[...], w_ref[...].astype(BF16), preferred_element_type=F32)
    o_ref[...] = (acc * cs_ref[...]).astype(o_ref.dtype)


def _project(x2d, g, w, col_scale, tm, tn):
    m, d = x2d.shape
    n = w.shape[1]
    return pl.pallas_call(
        _proj_kernel,
        name="proj",
        grid=(m // tm, n // tn),
        in_specs=[
            pl.BlockSpec((tm, d), lambda i, j: (i, 0)),
            pl.BlockSpec((1, d), lambda i, j: (0, 0)),
            pl.BlockSpec((d, tn), lambda i, j: (0, j)),
            pl.BlockSpec((1, tn), lambda i, j: (0, j)),
        ],
        out_specs=pl.BlockSpec((tm, tn), lambda i, j: (i, j)),
        out_shape=jax.ShapeDtypeStruct((m, n), BF16),
        scratch_shapes=[pltpu.VMEM((tm, d), BF16)],
        compiler_params=pltpu.CompilerParams(
            dimension_semantics=("parallel", "arbitrary"),
            vmem_limit_bytes=VMEM_LIMIT_V7X),
    )(x2d, g, w, col_scale)


def _t5_bucket_np(n):
    max_exact = N_BUCKETS // 2
    nf = np.maximum(n, 1).astype(np.float32)
    val = (np.log(nf / max_exact) / np.float32(math.log(MAX_DISTANCE / max_exact))
           * (N_BUCKETS - max_exact))
    large = np.minimum(max_exact + val.astype(np.int32), N_BUCKETS - 1)
    return np.where(n < max_exact, n, large).astype(np.int32)


def _bucket_tables():
    j = np.arange(MOBA_BLOCK)[:, None]
    i = np.arange(MOBA_BLOCK)[None, :]
    own = _t5_bucket_np(np.maximum(i - j, 0))
    own = np.where(j > i, -1, own)
    prev = _t5_bucket_np(MOBA_BLOCK + i - j)
    return np.stack([own, prev]).astype(np.int32)


def _bias_kernel(rb_ref, bkt_ref, o_ref):
    h = pl.program_id(0)
    bkt = bkt_ref[...]
    tab = jnp.where(bkt < 0, NEG_INF, 0.0).astype(F32)
    for b in range(N_BUCKETS):
        tab = jnp.where(bkt == b, rb_ref[b, h] * LOG2E, tab)
    o_ref[...] = tab


def _bias_tables(rel_bias):
    bkt = jnp.asarray(_bucket_tables())
    L = MOBA_BLOCK
    return pl.pallas_call(
        _bias_kernel,
        name="t5_bias",
        grid=(N_HEADS,),
        in_specs=[
            pl.BlockSpec(memory_space=pltpu.SMEM),
            pl.BlockSpec((2, L, L), lambda h: (0, 0, 0)),
        ],
        out_specs=pl.BlockSpec((None, 2, L, L), lambda h: (h, 0, 0, 0)),
        out_shape=jax.ShapeDtypeStruct((N_HEADS, 2, L, L), F32),
    )(rel_bias, bkt)


def _moba_kernel(rb_ref, q_ref, k_ref, v_ref, z_ref, tab_ref, o_ref, vt_ref, s_ref):
    L = MOBA_BLOCK
    nb = k_ref.shape[0] // L
    h = pl.program_id(1)

    kf = k_ref[...].astype(F32).reshape(nb, L, HEAD_DIM)
    km = jnp.mean(kf, axis=1)
    km_hi = km.astype(BF16)
    km_lo = (km - km_hi.astype(F32)).astype(BF16)
    km2 = jnp.concatenate([km_hi, km_lo], axis=0)
    vt_ref[:HEAD_DIM, :] = v_ref[...].T
    ones_row = lax.broadcasted_iota(jnp.int32, (VT_PAD, vt_ref.shape[1]), 0) == 0
    vt_ref[HEAD_DIM:, :] = jnp.where(ones_row, 1.0, 0.0).astype(BF16)
    far_bias = rb_ref[N_BUCKETS - 1, h] * LOG2E
    row = lax.broadcasted_iota(jnp.int32, (nb, L), 0)

    for qi in range(nb):
        q = q_ref[qi * L:(qi + 1) * L, :]
        sel_add = None
        if qi > MOBA_TOPK:
            g2 = lax.dot_general(km2, q, NT_DIMS, preferred_element_type=F32)
            g = g2[:nb] + g2[nb:]
            cnt = jnp.zeros((nb, L), jnp.int32)
            for mm in range(qi):
                gm = g[mm:mm + 1, :]
                ge = jnp.where(gm >= g, 1, 0)
                gt = jnp.where(gm > g, 1, 0)
                cnt = cnt + jnp.where(row > mm, ge, gt)
            sel_add = jnp.where(cnt < MOBA_TOPK, 0.0, NEG_INF)

        sbuf = s_ref.at[qi % 2]
        shifts = []
        m = None
        for n in range(qi + 1):
            s = lax.dot_general(k_ref[n * L:(n + 1) * L, :], q, NT_DIMS,
                                preferred_element_type=F32)
            shift = None
            if n == qi:
                s = s + tab_ref[0]
            else:
                if n == qi - 1:
                    s = s + tab_ref[1]
                if sel_add is not None:
                    shift = sel_add[n:n + 1, :]
                if n < qi - 1:
                    shift = far_bias if shift is None else shift + far_bias
            sbuf[n * L:(n + 1) * L, :] = s
            tmax = jnp.max(s, axis=0, keepdims=True)
            if shift is not None:
                tmax = tmax + shift
            shifts.append(shift)
            m = tmax if m is None else jnp.maximum(m, tmax)

        acc = None
        for n in range(qi + 1):
            off = m if shifts[n] is None else m - shifts[n]
            p = jnp.exp2((sbuf[n * L:(n + 1) * L, :] - off).astype(BF16))
            pv = jnp.dot(vt_ref[:, n * L:(n + 1) * L], p,
                         preferred_element_type=F32)
            acc = pv if acc is None else acc + pv
        l = acc[HEAD_DIM:HEAD_DIM + 1, :]
        y = (acc[:HEAD_DIM, :] * (1.0 / l)).T
        z = z_ref[qi * L:(qi + 1) * L, :].astype(F32)
        o_ref[qi * L:(qi + 1) * L, :] = (y * _silu(z)).astype(o_ref.dtype)


def _moba(proj3, tabs, rel_bias):
    B, S, _ = proj3.shape
    L = MOBA_BLOCK

    def head_cols(col0):
        return pl.BlockSpec((None, S, HEAD_DIM), lambda b, h: (b, 0, col0 + h))

    return pl.pallas_call(
        _moba_kernel,
        name="moba",
        grid=(B, N_HEADS),
        in_specs=[
            pl.BlockSpec(memory_space=pltpu.SMEM),
            head_cols(COL_QA), head_cols(COL_KA), head_cols(COL_VA), head_cols(COL_ZA),
            pl.BlockSpec((None, 2, L, L), lambda b, h: (h, 0, 0, 0)),
        ],
        out_specs=head_cols(0),
        out_shape=jax.ShapeDtypeStruct((B, S, D_A), BF16),
        scratch_shapes=[
            pltpu.VMEM((HEAD_DIM + VT_PAD, S), BF16),
            pltpu.VMEM((2, S, L), F32),
        ],
        compiler_params=pltpu.CompilerParams(
            dimension_semantics=("parallel", "parallel"),
            vmem_limit_bytes=VMEM_LIMIT_V7X),
    )(rel_bias, proj3, proj3, proj3, proj3, tabs)


def _retention_tables(S):
    C = RET_CHUNK
    dk = HEAD_DIM
    pos = jnp.arange(S, dtype=F32)
    theta = 1.0 / (ROT_BASE ** jnp.linspace(0.0, 1.0, dk // 2, dtype=F32))
    ang = pos[:, None] * theta[None, :]
    cos = jnp.repeat(jnp.cos(ang), 2, axis=1)
    sin = jnp.repeat(jnp.sin(ang), 2, axis=1) * jnp.tile(jnp.array([-1.0, 1.0], F32), dk // 2)
    log_gamma = jnp.log(1.0 - 2.0 ** (-5.0 - jnp.arange(N_HEADS, dtype=F32)))
    i = jnp.arange(C)
    diff = i[:, None] - i[None, :]
    k_scale = dk ** -0.5
    decay = jnp.where(diff >= 0,
                      jnp.exp(jnp.maximum(diff, 0).astype(F32) * log_gamma[:, None, None]),
                      0.0) * k_scale
    zeta = jnp.exp((C - 1 - i).astype(F32)[None, :] * log_gamma[:, None]) * k_scale
    xi = jnp.exp((i + 1).astype(F32)[None, :] * log_gamma[:, None])
    zeta = jnp.broadcast_to(zeta[:, :, None], (N_HEADS, C, dk))
    xi = jnp.broadcast_to(xi[:, :, None], (N_HEADS, C, dk))
    chunk_decay = jnp.exp(C * log_gamma)
    return cos, sin, decay, zeta, xi, chunk_decay


def _ret_kernel(cd_ref, q_ref, k_ref, v_ref, z_ref, cos_ref, sin_ref, dm_ref, zeta_ref, xi_ref,
                o_ref, r_ref):
    @pl.when(pl.program_id(1) == 0)
    def _():
        r_ref[...] = jnp.zeros_like(r_ref)

    cosv = cos_ref[...]
    sinv = sin_ref[...]
    lane = lax.broadcasted_iota(jnp.int32, cosv.shape, 1)
    even = (lane % 2) == 0

    def rot(t):
        partner = jnp.where(even, pltpu.roll(t, HEAD_DIM - 1, 1), pltpu.roll(t, 1, 1))
        return t * cosv + partner * sinv

    for h in range(N_HEADS):
        cols = slice(h * HEAD_DIM, (h + 1) * HEAD_DIM)
        q = rot(q_ref[:, cols].astype(F32))
        k = rot(k_ref[:, cols].astype(F32))
        v = v_ref[:, cols]
        qb = q.astype(BF16)
        inner = lax.dot_general(qb, k.astype(BF16), NT_DIMS,
                                preferred_element_type=F32) * dm_ref[h]
        r_old = r_ref[h]
        o = (jnp.dot(inner.astype(BF16), v, preferred_element_type=F32)
             + xi_ref[h] * jnp.dot(qb, r_old.astype(BF16), preferred_element_type=F32))
        kz = (k * zeta_ref[h]).astype(BF16)
        r_ref[h] = r_old * cd_ref[h] + lax.dot_general(kz, v, TN_DIMS,
                                                       preferred_element_type=F32)
        o = o * lax.rsqrt(jnp.mean(o * o, axis=-1, keepdims=True) + EPS)
        o_ref[:, cols] = (o * _silu(z_ref[:, cols].astype(F32))).astype(o_ref.dtype)


def _retention(proj3):
    B, S, _ = proj3.shape
    C = RET_CHUNK
    dk = HEAD_DIM
    cos, sin, decay, zeta, xi, chunk_decay = _retention_tables(S)

    def all_heads(col0):
        return pl.BlockSpec((None, C, D_A), lambda b, c: (b, c, col0))

    head_tab = pl.BlockSpec((N_HEADS, C, dk), lambda b, c: (0, 0, 0))
    return pl.pallas_call(
        _ret_kernel,
        name="retention",
        grid=(B, S // C),
        in_specs=[
            pl.BlockSpec(memory_space=pltpu.SMEM),
            all_heads(COL_QB), all_heads(COL_KB), all_heads(COL_VB), all_heads(COL_ZB),
            pl.BlockSpec((C, dk), lambda b, c: (c, 0)),
            pl.BlockSpec((C, dk), lambda b, c: (c, 0)),
            head_tab, head_tab, head_tab,
        ],
        out_specs=all_heads(0),
        out_shape=jax.ShapeDtypeStruct((B, S, D_A), BF16),
        scratch_shapes=[pltpu.VMEM((N_HEADS, dk, dk), F32)],
        compiler_params=pltpu.CompilerParams(
            dimension_semantics=("parallel", "arbitrary")),
    )(chunk_decay, proj3, proj3, proj3, proj3, cos, sin, decay, zeta, xi)


def _tail_kernel(ya_ref, yb_ref, ga_ref, gb_ref, x_ref, p_ref, wa_ref, wb_ref, wo_ref, wpg_ref,
                 wpp_ref, gple_ref, gfin_ref, o_ref):
    ua = jnp.dot(ya_ref[...], wa_ref[...], preferred_element_type=F32)
    ub = jnp.dot(yb_ref[...], wb_ref[...], preferred_element_type=F32)
    merged = (jax.nn.sigmoid(ga_ref[...].astype(F32)) * ua
              + jax.nn.sigmoid(gb_ref[...].astype(F32)) * ub)
    x1 = x_ref[...] + jnp.dot(merged.astype(BF16), wo_ref[...], preferred_element_type=F32)
    hp = _rms_norm(x1, gple_ref[...]).astype(BF16)
    gate = jax.nn.sigmoid(jnp.dot(hp, wpg_ref[...], preferred_element_type=F32))
    pe = jnp.dot(p_ref[...].astype(BF16), wpp_ref[...], preferred_element_type=F32)
    x2 = x1 + gate * pe
    o_ref[...] = _rms_norm(x2, gfin_ref[...])


def _tail(ya, yb, proj, x2d, p2d, wa, wb, wo, wpg, wpp, g_ple, g_final, tm):
    m, d = x2d.shape

    def const(shape):
        return pl.BlockSpec(shape, lambda i: (0, 0), pipeline_mode=pl.Buffered(1))

    return pl.pallas_call(
        _tail_kernel,
        name="tail",
        grid=(m // tm,),
        in_specs=[
            pl.BlockSpec((tm, D_A), lambda i: (i, 0)),
            pl.BlockSpec((tm, D_A), lambda i: (i, 0)),
            pl.BlockSpec((tm, d), lambda i: (i, COL_GA)),
            pl.BlockSpec((tm, d), lambda i: (i, COL_GB)),
            pl.BlockSpec((tm, d), lambda i: (i, 0)),
            pl.BlockSpec((tm, D_PLE), lambda i: (i, 0)),
            const((D_A, d)), const((D_A, d)), const((d, d)), const((d, d)), const((D_PLE, d)),
            const((1, d)), const((1, d)),
        ],
        out_specs=pl.BlockSpec((tm, d), lambda i: (i, 0)),
        out_shape=jax.ShapeDtypeStruct((m, d), F32),
        compiler_params=pltpu.CompilerParams(
            dimension_semantics=("parallel",),
            vmem_limit_bytes=VMEM_LIMIT_V7X),
    )(ya, yb, proj, proj, x2d, p2d, wa, wb, wo, wpg, wpp, g_ple, g_final)


def kernel(x, p, g_mix, w_in, w_a, w_b, w_out, g_ple, w_ple_gate, w_ple_proj, rel_bias, g_final):
    B, S, d = x.shape
    assert w_in.shape[0] == 1, "the fused tail applies the final norm: single-layer stacks only"
    tabs = _bias_tables(rel_bias)
    x2d = x.reshape(B * S, d)
    col_scale = jnp.where(jnp.arange(D_IN) < D_A, HEAD_DIM ** -0.5 * LOG2E, 1.0).astype(F32)
    proj = _project(x2d, g_mix, w_in[0], col_scale[None, :], tm=1024, tn=512)
    proj3 = proj.reshape(B, S, D_IN)
    ya = _moba(proj3, tabs, rel_bias).reshape(B * S, D_A)
    yb = _retention(proj3).reshape(B * S, D_A)
    out = _tail(ya, yb, proj, x2d, p[0].reshape(B * S, D_PLE),
                w_a[0].astype(BF16), w_b[0].astype(BF16), w_out[0].astype(BF16),
                w_ple_gate[0].astype(BF16), w_ple_proj[0].astype(BF16),
                g_ple, g_final[None, :], tm=256)
    return out.reshape(B, S, d)
```

```python
import math

import numpy as np
import jax
import jax.numpy as jnp
from jax import lax
from jax.experimental import pallas as pl
from jax.experimental.pallas import tpu as pltpu

D_MODEL = 2048
N_HEADS = 8
HEAD_DIM = 128
D_A = N_HEADS * HEAD_DIM
MOBA_BLOCK = 256
MOBA_TOPK = 3
N_BUCKETS = 32
MAX_DISTANCE = 128
RET_CHUNK = 128
ROT_BASE = 10000.0
D_PLE = 256
EPS = 1e-6
D_IN = 8 * D_A + 2 * D_MODEL

COL_QA, COL_KA, COL_VA, COL_ZA = (i * N_HEADS for i in range(4))
COL_QB, COL_KB, COL_VB, COL_ZB = 4, 5, 6, 7
COL_GA = 8 * D_A // D_MODEL
COL_GB = COL_GA + 1

VMEM_LIMIT_V7X = 56 * 1024 * 1024
BF16_SUBLANES = 16
VT_PAD = BF16_SUBLANES
MOBA_GROUPS = ((7, 0), (6, 1), (5, 2), (4, 3))

F32 = jnp.float32
BF16 = jnp.bfloat16
NEG_INF = float("-inf")
LOG2E = math.log2(math.e)
NT_DIMS = (((1,), (1,)), ((), ()))
TN_DIMS = (((0,), (0,)), ((), ()))


def _rms_norm(x, g):
    return x * lax.rsqrt(jnp.mean(x * x, axis=-1, keepdims=True) + EPS) * g


def _silu(z):
    return z * jax.nn.sigmoid(z)


def _round_robin(gens):
    gens = list(gens)
    while gens:
        for g in list(gens):
            try:
                next(g)
            except StopIteration:
                gens.remove(g)


def _proj_kernel(x_ref, g_ref, w_ref, cs_ref, o_ref, h_ref):
    @pl.when(pl.program_id(1) == 0)
    def _():
        h_ref[...] = _rms_norm(x_ref[...], g_ref[...]).astype(BF16)

    acc = jnp.dot(h_ref[...], w_ref[...].astype(BF16), preferred_element_type=F32)
    o_ref[...] = (acc * cs_ref[...]).astype(o_ref.dtype)


def _project(x2d, g, w, col_scale, tm, tn):
    m, d = x2d.shape
    n = w.shape[1]
    return pl.pallas_call(
        _proj_kernel,
        name="proj",
        grid=(m // tm, n // tn),
        in_specs=[
            pl.BlockSpec((tm, d), lambda i, j: (i, 0)),
            pl.BlockSpec((1, d), lambda i, j: (0, 0)),
            pl.BlockSpec((d, tn), lambda i, j: (0, j)),
            pl.BlockSpec((1, tn), lambda i, j: (0, j)),
        ],
        out_specs=pl.BlockSpec((tm, tn), lambda i, j: (i, j)),
        out_shape=jax.ShapeDtypeStruct((m, n), BF16),
        scratch_shapes=[pltpu.VMEM((tm, d), BF16)],
        compiler_params=pltpu.CompilerParams(
            dimension_semantics=("parallel", "arbitrary"),
            vmem_limit_bytes=VMEM_LIMIT_V7X),
    )(x2d, g, w, col_scale)


def _t5_bucket_np(n):
    max_exact = N_BUCKETS // 2
    nf = np.maximum(n, 1).astype(np.float32)
    val = (np.log(nf / max_exact) / np.float32(math.log(MAX_DISTANCE / max_exact))
           * (N_BUCKETS - max_exact))
    large = np.minimum(max_exact + val.astype(np.int32), N_BUCKETS - 1)
    return np.where(n < max_exact, n, large).astype(np.int32)


def _bucket_tables():
    j = np.arange(MOBA_BLOCK)[:, None]
    i = np.arange(MOBA_BLOCK)[None, :]
    own = _t5_bucket_np(np.maximum(i - j, 0))
    own = np.where(j > i, -1, own)
    prev = _t5_bucket_np(MOBA_BLOCK + i - j)
    return np.stack([own, prev]).astype(np.int32)


def _bias_kernel(rb_ref, bkt_ref, o_ref):
    h = pl.program_id(0)
    bkt = bkt_ref[...]
    tab = jnp.where(bkt < 0, NEG_INF, 0.0).astype(F32)
    for b in range(N_BUCKETS):
        tab = jnp.where(bkt == b, rb_ref[b, h] * LOG2E, tab)
    o_ref[...] = tab


def _bias_tables(rel_bias):
    bkt = jnp.asarray(_bucket_tables())
    L = MOBA_BLOCK
    return pl.pallas_call(
        _bias_kernel,
        name="t5_bias",
        grid=(N_HEADS,),
        in_specs=[
            pl.BlockSpec(memory_space=pltpu.SMEM),
            pl.BlockSpec((2, L, L), lambda h: (0, 0, 0)),
        ],
        out_specs=pl.BlockSpec((None, 2, L, L), lambda h: (h, 0, 0, 0)),
        out_shape=jax.ShapeDtypeStruct((N_HEADS, 2, L, L), F32),
    )(rel_bias, bkt)


def _moba_kernel(rb_ref, q_ref, k_ref, v_ref, z_ref, tab_ref, o_ref, vt_ref, s_ref):
    L = MOBA_BLOCK
    nb = k_ref.shape[0] // L
    h = pl.program_id(1)

    kf = k_ref[...].astype(F32).reshape(nb, L, HEAD_DIM)
    km = jnp.mean(kf, axis=1)
    km_hi = km.astype(BF16)
    km_lo = (km - km_hi.astype(F32)).astype(BF16)
    km2 = jnp.concatenate([km_hi, km_lo], axis=0)
    vt_ref[:HEAD_DIM, :] = v_ref[...].T
    ones_row = lax.broadcasted_iota(jnp.int32, (VT_PAD, vt_ref.shape[1]), 0) == 0
    vt_ref[HEAD_DIM:, :] = jnp.where(ones_row, 1.0, 0.0).astype(BF16)
    far_bias = rb_ref[N_BUCKETS - 1, h] * LOG2E
    row = lax.broadcasted_iota(jnp.int32, (nb, L), 0)

    def scores(qi, slot, out):
        q = q_ref[qi * L:(qi + 1) * L, :]
        sel_add = None
        if qi > MOBA_TOPK:
            g2 = lax.dot_general(km2, q, NT_DIMS, preferred_element_type=F32)
            g = g2[:nb] + g2[nb:]
            cnt = jnp.zeros((nb, L), jnp.int32)
            for mm in range(qi):
                gm = g[mm:mm + 1, :]
                ge = jnp.where(gm >= g, 1, 0)
                gt = jnp.where(gm > g, 1, 0)
                cnt = cnt + jnp.where(row > mm, ge, gt)
            sel_add = jnp.where(cnt < MOBA_TOPK, 0.0, NEG_INF)

        sbuf = s_ref.at[slot]
        shifts = []
        m = None
        for n in range(qi + 1):
            s = lax.dot_general(k_ref[n * L:(n + 1) * L, :], q, NT_DIMS,
                                preferred_element_type=F32)
            shift = None
            if n == qi:
                s = s + tab_ref[0]
            else:
                if n == qi - 1:
                    s = s + tab_ref[1]
                if sel_add is not None:
                    shift = sel_add[n:n + 1, :]
                if n < qi - 1:
                    shift = far_bias if shift is None else shift + far_bias
            sbuf[n * L:(n + 1) * L, :] = s
            tmax = jnp.max(s, axis=0, keepdims=True)
            if shift is not None:
                tmax = tmax + shift
            shifts.append(shift)
            m = tmax if m is None else jnp.maximum(m, tmax)
            yield
        out.append((m, shifts))

    def outputs(qi, slot, m, shifts):
        sbuf = s_ref.at[slot]
        acc = None
        for n in range(qi + 1):
            off = m if shifts[n] is None else m - shifts[n]
            p = jnp.exp2((sbuf[n * L:(n + 1) * L, :] - off).astype(BF16))
            pv = jnp.dot(vt_ref[:, n * L:(n + 1) * L], p,
                         preferred_element_type=F32)
            acc = pv if acc is None else acc + pv
            yield
        l = acc[HEAD_DIM:HEAD_DIM + 1, :]
        y = (acc[:HEAD_DIM, :] * (1.0 / l)).T
        z = z_ref[qi * L:(qi + 1) * L, :].astype(F32)
        o_ref[qi * L:(qi + 1) * L, :] = (y * _silu(z)).astype(o_ref.dtype)

    width = len(MOBA_GROUPS[0])
    pending = []
    for t, grp in enumerate(MOBA_GROUPS):
        outs = [[] for _ in grp]
        slots = [(t % 2) * width + i for i in range(width)]
        _round_robin([scores(qi, slots[i], outs[i]) for i, qi in enumerate(grp)]
                     + [outputs(*args) for args in pending])
        pending = [(qi, slots[i]) + outs[i][0] for i, qi in enumerate(grp)]
    _round_robin([outputs(*args) for args in pending])


def _moba(proj3, tabs, rel_bias):
    B, S, _ = proj3.shape
    L = MOBA_BLOCK
    assert sorted(qi for grp in MOBA_GROUPS for qi in grp) == list(range(S // L))

    def head_cols(col0):
        return pl.BlockSpec((None, S, HEAD_DIM), lambda b, h: (b, 0, col0 + h))

    return pl.pallas_call(
        _moba_kernel,
        name="moba",
        grid=(B, N_HEADS),
        in_specs=[
            pl.BlockSpec(memory_space=pltpu.SMEM),
            head_cols(COL_QA), head_cols(COL_KA), head_cols(COL_VA), head_cols(COL_ZA),
            pl.BlockSpec((None, 2, L, L), lambda b, h: (h, 0, 0, 0)),
        ],
        out_specs=head_cols(0),
        out_shape=jax.ShapeDtypeStruct((B, S, D_A), BF16),
        scratch_shapes=[
            pltpu.VMEM((HEAD_DIM + VT_PAD, S), BF16),
            pltpu.VMEM((2 * len(MOBA_GROUPS[0]), S, L), F32),
        ],
        compiler_params=pltpu.CompilerParams(
            dimension_semantics=("parallel", "parallel"),
            vmem_limit_bytes=VMEM_LIMIT_V7X),
    )(rel_bias, proj3, proj3, proj3, proj3, tabs)


def _retention_tables(S):
    C = RET_CHUNK
    dk = HEAD_DIM
    pos = jnp.arange(S, dtype=F32)
    theta = 1.0 / (ROT_BASE ** jnp.linspace(0.0, 1.0, dk // 2, dtype=F32))
    ang = pos[:, None] * theta[None, :]
    cos = jnp.repeat(jnp.cos(ang), 2, axis=1)
    sin = jnp.repeat(jnp.sin(ang), 2, axis=1) * jnp.tile(jnp.array([-1.0, 1.0], F32), dk // 2)
    log_gamma = jnp.log(1.0 - 2.0 ** (-5.0 - jnp.arange(N_HEADS, dtype=F32)))
    i = jnp.arange(C)
    diff = i[:, None] - i[None, :]
    k_scale = dk ** -0.5
    decay = jnp.where(diff >= 0,
                      jnp.exp(jnp.maximum(diff, 0).astype(F32) * log_gamma[:, None, None]),
                      0.0) * k_scale
    zeta = jnp.exp((C - 1 - i).astype(F32)[None, :] * log_gamma[:, None]) * k_scale
    xi = jnp.exp((i + 1).astype(F32)[None, :] * log_gamma[:, None])
    zeta = jnp.broadcast_to(zeta[:, :, None], (N_HEADS, C, dk))
    xi = jnp.broadcast_to(xi[:, :, None], (N_HEADS, C, dk))
    chunk_decay = jnp.exp(C * log_gamma)
    return cos, sin, decay, zeta, xi, chunk_decay


def _ret_kernel(cd_ref, q_ref, k_ref, v_ref, z_ref, cos_ref, sin_ref, dm_ref, zeta_ref, xi_ref,
                o_ref, r_ref):
    @pl.when(pl.program_id(1) == 0)
    def _():
        r_ref[...] = jnp.zeros_like(r_ref)

    cosv = cos_ref[...]
    sinv = sin_ref[...]
    lane = lax.broadcasted_iota(jnp.int32, cosv.shape, 1)
    even = (lane % 2) == 0

    def rot(t):
        partner = jnp.where(even, pltpu.roll(t, HEAD_DIM - 1, 1), pltpu.roll(t, 1, 1))
        return t * cosv + partner * sinv

    def head(h):
        cols = slice(h * HEAD_DIM, (h + 1) * HEAD_DIM)
        q = rot(q_ref[:, cols].astype(F32))
        k = rot(k_ref[:, cols].astype(F32))
        v = v_ref[:, cols]
        qb = q.astype(BF16)
        yield
        inner = lax.dot_general(qb, k.astype(BF16), NT_DIMS,
                                preferred_element_type=F32) * dm_ref[h]
        r_old = r_ref[h]
        cross = xi_ref[h] * jnp.dot(qb, r_old.astype(BF16), preferred_element_type=F32)
        kz = (k * zeta_ref[h]).astype(BF16)
        r_ref[h] = r_old * cd_ref[h] + lax.dot_general(kz, v, TN_DIMS,
                                                       preferred_element_type=F32)
        yield
        o = jnp.dot(inner.astype(BF16), v, preferred_element_type=F32) + cross
        yield
        o = o * lax.rsqrt(jnp.mean(o * o, axis=-1, keepdims=True) + EPS)
        o_ref[:, cols] = (o * _silu(z_ref[:, cols].astype(F32))).astype(o_ref.dtype)

    _round_robin([head(h) for h in range(N_HEADS)])


def _retention(proj3):
    B, S, _ = proj3.shape
    C = RET_CHUNK
    dk = HEAD_DIM
    cos, sin, decay, zeta, xi, chunk_decay = _retention_tables(S)

    def all_heads(col0):
        return pl.BlockSpec((None, C, D_A), lambda b, c: (b, c, col0))

    head_tab = pl.BlockSpec((N_HEADS, C, dk), lambda b, c: (0, 0, 0))
    return pl.pallas_call(
        _ret_kernel,
        name="retention",
        grid=(B, S // C),
        in_specs=[
            pl.BlockSpec(memory_space=pltpu.SMEM),
            all_heads(COL_QB), all_heads(COL_KB), all_heads(COL_VB), all_heads(COL_ZB),
            pl.BlockSpec((C, dk), lambda b, c: (c, 0)),
            pl.BlockSpec((C, dk), lambda b, c: (c, 0)),
            head_tab, head_tab, head_tab,
        ],
        out_specs=all_heads(0),
        out_shape=jax.ShapeDtypeStruct((B, S, D_A), BF16),
        scratch_shapes=[pltpu.VMEM((N_HEADS, dk, dk), F32)],
        compiler_params=pltpu.CompilerParams(
            dimension_semantics=("parallel", "arbitrary")),
    )(chunk_decay, proj3, proj3, proj3, proj3, cos, sin, decay, zeta, xi)


def _tail_kernel(ya_ref, yb_ref, ga_ref, gb_ref, x_ref, p_ref, wa_ref, wb_ref, wo_ref, wpg_ref,
                 wpp_ref, gple_ref, gfin_ref, o_ref):
    ua = jnp.dot(ya_ref[...], wa_ref[...], preferred_element_type=F32)
    ub = jnp.dot(yb_ref[...], wb_ref[...], preferred_element_type=F32)
    merged = (jax.nn.sigmoid(ga_ref[...].astype(F32)) * ua
              + jax.nn.sigmoid(gb_ref[...].astype(F32)) * ub)
    x1 = x_ref[...] + jnp.dot(merged.astype(BF16), wo_ref[...], preferred_element_type=F32)
    hp = _rms_norm(x1, gple_ref[...]).astype(BF16)
    gate = jax.nn.sigmoid(jnp.dot(hp, wpg_ref[...], preferred_element_type=F32))
    pe = jnp.dot(p_ref[...].astype(BF16), wpp_ref[...], preferred_element_type=F32)
    x2 = x1 + gate * pe
    o_ref[...] = _rms_norm(x2, gfin_ref[...])


def _tail(ya, yb, proj, x2d, p2d, wa, wb, wo, wpg, wpp, g_ple, g_final, tm):
    m, d = x2d.shape

    def const(shape):
        return pl.BlockSpec(shape, lambda i: (0, 0), pipeline_mode=pl.Buffered(1))

    return pl.pallas_call(
        _tail_kernel,
        name="tail",
        grid=(m // tm,),
        in_specs=[
            pl.BlockSpec((tm, D_A), lambda i: (i, 0)),
            pl.BlockSpec((tm, D_A), lambda i: (i, 0)),
            pl.BlockSpec((tm, d), lambda i: (i, COL_GA)),
            pl.BlockSpec((tm, d), lambda i: (i, COL_GB)),
            pl.BlockSpec((tm, d), lambda i: (i, 0)),
            pl.BlockSpec((tm, D_PLE), lambda i: (i, 0)),
            const((D_A, d)), const((D_A, d)), const((d, d)), const((d, d)), const((D_PLE, d)),
            const((1, d)), const((1, d)),
        ],
        out_specs=pl.BlockSpec((tm, d), lambda i: (i, 0)),
        out_shape=jax.ShapeDtypeStruct((m, d), F32),
        compiler_params=pltpu.CompilerParams(
            dimension_semantics=("parallel",),
            vmem_limit_bytes=VMEM_LIMIT_V7X),
    )(ya, yb, proj, proj, x2d, p2d, wa, wb, wo, wpg, wpp, g_ple, g_final)


def kernel(x, p, g_mix, w_in, w_a, w_b, w_out, g_ple, w_ple_gate, w_ple_proj, rel_bias, g_final):
    B, S, d = x.shape
    assert w_in.shape[0] == 1, "the fused tail applies the final norm: single-layer stacks only"
    tabs = _bias_tables(rel_bias)
    x2d = x.reshape(B * S, d)
    col_scale = jnp.where(jnp.arange(D_IN) < D_A, HEAD_DIM ** -0.5 * LOG2E, 1.0).astype(F32)
    proj = _project(x2d, g_mix, w_in[0], col_scale[None, :], tm=1024, tn=1024)
    proj3 = proj.reshape(B, S, D_IN)
    ya = _moba(proj3, tabs, rel_bias).reshape(B * S, D_A)
    yb = _retention(proj3).reshape(B * S, D_A)
    out = _tail(ya, yb, proj, x2d, p[0].reshape(B * S, D_PLE),
                w_a[0].astype(BF16), w_b[0].astype(BF16), w_out[0].astype(BF16),
                w_ple_gate[0].astype(BF16), w_ple_proj[0].astype(BF16),
                g_ple, g_final[None, :], tm=256)
    return out.reshape(B, S, d)
```

```python
import functools
import math

import numpy as np
import jax
import jax.numpy as jnp
from jax import lax
from jax.experimental import pallas as pl
from jax.experimental.pallas import tpu as pltpu

D_MODEL = 2048
N_HEADS = 8
HEAD_DIM = 128
D_A = N_HEADS * HEAD_DIM
MOBA_BLOCK = 256
MOBA_TOPK = 3
N_BUCKETS = 32
MAX_DISTANCE = 128
RET_CHUNK = 128
ROT_BASE = 10000.0
D_PLE = 256
EPS = 1e-6
D_IN = 8 * D_A + 2 * D_MODEL

COL_QA, COL_KA, COL_VA, COL_ZA = (i * N_HEADS for i in range(4))
COL_QB, COL_KB, COL_VB, COL_ZB = 4, 5, 6, 7
COL_GA = 8 * D_A // D_MODEL
COL_GB = COL_GA + 1

VMEM_LIMIT_V7X = 56 * 1024 * 1024
BF16_SUBLANES = 16
VT_PAD = BF16_SUBLANES
MOBA_GROUPS = ((7, 0), (6, 1), (5, 2), (4, 3))
RET_LANES = 2

F32 = jnp.float32
BF16 = jnp.bfloat16
NEG_INF = float("-inf")
LOG2E = math.log2(math.e)
NT_DIMS = (((1,), (1,)), ((), ()))
TN_DIMS = (((0,), (0,)), ((), ()))


def _rms_norm(x, g):
    return x * lax.rsqrt(jnp.mean(x * x, axis=-1, keepdims=True) + EPS) * g


def _silu(z):
    return z * jax.nn.sigmoid(z)


def _sweeps(gens):
    gens = list(gens)
    while gens:
        for g in list(gens):
            try:
                next(g)
            except StopIteration:
                gens.remove(g)
        yield


def _round_robin(gens):
    for _ in _sweeps(gens):
        pass


def _chain(gens):
    for g in gens:
        yield from g


def _proj_kernel(x_ref, g_ref, w_ref, cs_ref, o_ref, h_ref):
    @pl.when(pl.program_id(1) == 0)
    def _():
        h_ref[...] = _rms_norm(x_ref[...], g_ref[...]).astype(BF16)

    acc = jnp.dot(h_ref[...], w_ref[...].astype(BF16), preferred_element_type=F32)
    o_ref[...] = (acc * cs_ref[...]).astype(o_ref.dtype)


def _project(x2d, g, w, col_scale, tm, tn):
    m, d = x2d.shape
    n = w.shape[1]
    return pl.pallas_call(
        _proj_kernel,
        name="proj",
        grid=(m // tm, n // tn),
        in_specs=[
            pl.BlockSpec((tm, d), lambda i, j: (i, 0)),
            pl.BlockSpec((1, d), lambda i, j: (0, 0)),
            pl.BlockSpec((d, tn), lambda i, j: (0, j)),
            pl.BlockSpec((1, tn), lambda i, j: (0, j)),
        ],
        out_specs=pl.BlockSpec((tm, tn), lambda i, j: (i, j)),
        out_shape=jax.ShapeDtypeStruct((m, n), BF16),
        scratch_shapes=[pltpu.VMEM((tm, d), BF16)],
        compiler_params=pltpu.CompilerParams(
            dimension_semantics=("parallel", "arbitrary"),
            vmem_limit_bytes=VMEM_LIMIT_V7X),
    )(x2d, g, w, col_scale)


def _t5_bucket_np(n):
    max_exact = N_BUCKETS // 2
    nf = np.maximum(n, 1).astype(np.float32)
    val = (np.log(nf / max_exact) / np.float32(math.log(MAX_DISTANCE / max_exact))
           * (N_BUCKETS - max_exact))
    large = np.minimum(max_exact + val.astype(np.int32), N_BUCKETS - 1)
    return np.where(n < max_exact, n, large).astype(np.int32)


def _bucket_tables():
    j = np.arange(MOBA_BLOCK)[:, None]
    i = np.arange(MOBA_BLOCK)[None, :]
    own = _t5_bucket_np(np.maximum(i - j, 0))
    own = np.where(j > i, -1, own)
    prev = _t5_bucket_np(MOBA_BLOCK + i - j)
    return np.stack([own, prev]).astype(np.int32)


def _bias_kernel(rb_ref, bkt_ref, o_ref):
    h = pl.program_id(0)
    bkt = bkt_ref[...]
    tab = jnp.where(bkt < 0, NEG_INF, 0.0).astype(F32)
    for b in range(N_BUCKETS):
        tab = jnp.where(bkt == b, rb_ref[b, h] * LOG2E, tab)
    o_ref[...] = tab


def _bias_tables(rel_bias):
    bkt = jnp.asarray(_bucket_tables())
    L = MOBA_BLOCK
    return pl.pallas_call(
        _bias_kernel,
        name="t5_bias",
        grid=(N_HEADS,),
        in_specs=[
            pl.BlockSpec(memory_space=pltpu.SMEM),
            pl.BlockSpec((2, L, L), lambda h: (0, 0, 0)),
        ],
        out_specs=pl.BlockSpec((None, 2, L, L), lambda h: (h, 0, 0, 0)),
        out_shape=jax.ShapeDtypeStruct((N_HEADS, 2, L, L), F32),
    )(rel_bias, bkt)


def _moba_sweeps(rb_ref, q_ref, k_ref, v_ref, z_ref, tab_ref, o_ref, vt_ref, s_ref):
    L = MOBA_BLOCK
    nb = k_ref.shape[0] // L
    h = pl.program_id(1)

    kf = k_ref[...].astype(F32).reshape(nb, L, HEAD_DIM)
    km = jnp.mean(kf, axis=1)
    km_hi = km.astype(BF16)
    km_lo = (km - km_hi.astype(F32)).astype(BF16)
    km2 = jnp.concatenate([km_hi, km_lo], axis=0)
    vt_ref[:HEAD_DIM, :] = v_ref[...].T
    ones_row = lax.broadcasted_iota(jnp.int32, (VT_PAD, vt_ref.shape[1]), 0) == 0
    vt_ref[HEAD_DIM:, :] = jnp.where(ones_row, 1.0, 0.0).astype(BF16)
    far_bias = rb_ref[N_BUCKETS - 1, h] * LOG2E
    row = lax.broadcasted_iota(jnp.int32, (nb, L), 0)

    def scores(qi, slot, out):
        q = q_ref[qi * L:(qi + 1) * L, :]
        sel_add = None
        if qi > MOBA_TOPK:
            g2 = lax.dot_general(km2, q, NT_DIMS, preferred_element_type=F32)
            g = g2[:nb] + g2[nb:]
            cnt = jnp.zeros((nb, L), jnp.int32)
            for mm in range(qi):
                gm = g[mm:mm + 1, :]
                ge = jnp.where(gm >= g, 1, 0)
                gt = jnp.where(gm > g, 1, 0)
                cnt = cnt + jnp.where(row > mm, ge, gt)
            sel_add = jnp.where(cnt < MOBA_TOPK, 0.0, NEG_INF)

        sbuf = s_ref.at[slot]
        shifts = []
        m = None
        for n in range(qi + 1):
            s = lax.dot_general(k_ref[n * L:(n + 1) * L, :], q, NT_DIMS,
                                preferred_element_type=F32)
            shift = None
            if n == qi:
                s = s + tab_ref[0]
            else:
                if n == qi - 1:
                    s = s + tab_ref[1]
                if sel_add is not None:
                    shift = sel_add[n:n + 1, :]
                if n < qi - 1:
                    shift = far_bias if shift is None else shift + far_bias
            sbuf[n * L:(n + 1) * L, :] = s
            tmax = jnp.max(s, axis=0, keepdims=True)
            if shift is not None:
                tmax = tmax + shift
            shifts.append(shift)
            m = tmax if m is None else jnp.maximum(m, tmax)
            yield
        out.append((m, shifts))

    def outputs(qi, slot, m, shifts):
        sbuf = s_ref.at[slot]
        acc = None
        for n in range(qi + 1):
            off = m if shifts[n] is None else m - shifts[n]
            p = jnp.exp2((sbuf[n * L:(n + 1) * L, :] - off).astype(BF16))
            pv = jnp.dot(vt_ref[:, n * L:(n + 1) * L], p,
                         preferred_element_type=F32)
            acc = pv if acc is None else acc + pv
            yield
        l = acc[HEAD_DIM:HEAD_DIM + 1, :]
        y = (acc[:HEAD_DIM, :] * (1.0 / l)).T
        z = z_ref[qi * L:(qi + 1) * L, :].astype(F32)
        o_ref[qi * L:(qi + 1) * L, :] = (y * _silu(z)).astype(o_ref.dtype)

    width = len(MOBA_GROUPS[0])
    pending = []
    for t, grp in enumerate(MOBA_GROUPS):
        outs = [[] for _ in grp]
        slots = [(t % 2) * width + i for i in range(width)]
        yield from _sweeps([scores(qi, slots[i], outs[i]) for i, qi in enumerate(grp)]
                           + [outputs(*args) for args in pending])
        pending = [(qi, slots[i]) + outs[i][0] for i, qi in enumerate(grp)]
    yield from _sweeps([outputs(*args) for args in pending])


def _retention_tables(S):
    C = RET_CHUNK
    dk = HEAD_DIM
    pos = jnp.arange(S, dtype=F32)
    theta = 1.0 / (ROT_BASE ** jnp.linspace(0.0, 1.0, dk // 2, dtype=F32))
    ang = pos[:, None] * theta[None, :]
    cos = jnp.repeat(jnp.cos(ang), 2, axis=1)
    sin = jnp.repeat(jnp.sin(ang), 2, axis=1) * jnp.tile(jnp.array([-1.0, 1.0], F32), dk // 2)
    log_gamma = jnp.log(1.0 - 2.0 ** (-5.0 - jnp.arange(N_HEADS, dtype=F32)))
    i = jnp.arange(C)
    diff = i[:, None] - i[None, :]
    k_scale = dk ** -0.5
    decay = jnp.where(diff >= 0,
                      jnp.exp(jnp.maximum(diff, 0).astype(F32) * log_gamma[:, None, None]),
                      0.0) * k_scale
    zeta = jnp.exp((C - 1 - i).astype(F32)[None, :] * log_gamma[:, None]) * k_scale
    xi = jnp.exp((i + 1).astype(F32)[None, :] * log_gamma[:, None])
    zeta = jnp.broadcast_to(zeta[:, :, None], (N_HEADS, C, dk))
    xi = jnp.broadcast_to(xi[:, :, None], (N_HEADS, C, dk))
    chunk_decay = jnp.exp(C * log_gamma)
    return cos, sin, decay, zeta, xi, chunk_decay


def _ret_chains(cd_ref, q_ref, k_ref, v_ref, z_ref, cos_ref, sin_ref, dm_ref, zeta_ref, xi_ref,
                o_ref, r_ref, first):
    C = RET_CHUNK
    n_chunks = q_ref.shape[0] // C

    @pl.when(first)
    def _():
        r_ref[...] = jnp.zeros_like(r_ref)

    lane = lax.broadcasted_iota(jnp.int32, (C, HEAD_DIM), 1)
    even = (lane % 2) == 0

    def rot(t, cosv, sinv):
        partner = jnp.where(even, pltpu.roll(t, HEAD_DIM - 1, 1), pltpu.roll(t, 1, 1))
        return t * cosv + partner * sinv

    def head(h):
        cols = slice(h * HEAD_DIM, (h + 1) * HEAD_DIM)
        for c in range(n_chunks):
            rows = slice(c * C, (c + 1) * C)
            cosv = cos_ref[rows, :]
            sinv = sin_ref[rows, :]
            q = rot(q_ref[rows, cols].astype(F32), cosv, sinv)
            k = rot(k_ref[rows, cols].astype(F32), cosv, sinv)
            v = v_ref[rows, cols]
            qb = q.astype(BF16)
            yield
            inner = lax.dot_general(qb, k.astype(BF16), NT_DIMS,
                                    preferred_element_type=F32) * dm_ref[h]
            r_old = r_ref[h]
            cross = xi_ref[h] * jnp.dot(qb, r_old.astype(BF16), preferred_element_type=F32)
            kz = (k * zeta_ref[h]).astype(BF16)
            r_ref[h] = r_old * cd_ref[h] + lax.dot_general(kz, v, TN_DIMS,
                                                           preferred_element_type=F32)
            yield
            o = jnp.dot(inner.astype(BF16), v, preferred_element_type=F32) + cross
            yield
            o = o * lax.rsqrt(jnp.mean(o * o, axis=-1, keepdims=True) + EPS)
            o_ref[rows, cols] = (o * _silu(z_ref[rows, cols].astype(F32))).astype(o_ref.dtype)
            yield

    return [head(h) for h in range(N_HEADS)]


def _attn_kernel(n_cast, rb_ref, qa_ref, ka_ref, va_ref, za_ref, tab_ref,
                 cd_ref, qb_ref, kb_ref, vb_ref, zb_ref, cos_ref, sin_ref, dm_ref, zeta_ref, xi_ref,
                 *refs):
    w32_refs = refs[:n_cast]
    ya_ref, yb_ref = refs[n_cast], refs[n_cast + 1]
    w16_refs = refs[n_cast + 2:2 * n_cast + 2]
    vt_ref, s_ref, r_ref = refs[2 * n_cast + 2:]

    def casts():
        for w32, w16 in zip(w32_refs, w16_refs):
            w16[...] = w32[...].astype(BF16)
            yield

    heads = _ret_chains(cd_ref, qb_ref, kb_ref, vb_ref, zb_ref, cos_ref, sin_ref, dm_ref, zeta_ref,
                        xi_ref, yb_ref, r_ref, first=pl.program_id(1) == 0)
    lanes = [_chain(heads[i::RET_LANES]) for i in range(RET_LANES)]
    moba = _moba_sweeps(rb_ref, qa_ref, ka_ref, va_ref, za_ref, tab_ref, ya_ref, vt_ref, s_ref)
    _round_robin([moba] + lanes + [casts()])


def _attention(proj3, tabs, rel_bias, weights):
    B, S, _ = proj3.shape
    L = MOBA_BLOCK
    C = RET_CHUNK
    dk = HEAD_DIM
    rows_b = S // N_HEADS
    assert sorted(qi for grp in MOBA_GROUPS for qi in grp) == list(range(S // L))
    assert rows_b % C == 0
    n_steps = B * N_HEADS
    cos, sin, decay, zeta, xi, chunk_decay = _retention_tables(S)

    def head_cols(col0):
        return pl.BlockSpec((None, S, HEAD_DIM), lambda b, h: (b, 0, col0 + h))

    def all_heads(col0):
        return pl.BlockSpec((None, rows_b, D_A), lambda b, h: (b, h, col0))

    def w_slice(w):
        rows = w.shape[0] // n_steps
        assert rows * n_steps == w.shape[0] and rows % BF16_SUBLANES == 0, w.shape
        return pl.BlockSpec((rows, w.shape[1]), lambda b, h: (b * N_HEADS + h, 0))

    head_tab = pl.BlockSpec((N_HEADS, C, dk), lambda b, h: (0, 0, 0))
    w_specs = [w_slice(w) for w in weights]
    smem = pl.BlockSpec(memory_space=pltpu.SMEM)
    outs = pl.pallas_call(
        functools.partial(_attn_kernel, len(weights)),
        name="attention",
        grid=(B, N_HEADS),
        in_specs=[
            smem,
            head_cols(COL_QA), head_cols(COL_KA), head_cols(COL_VA), head_cols(COL_ZA),
            pl.BlockSpec((None, 2, L, L), lambda b, h: (h, 0, 0, 0)),
            smem,
            all_heads(COL_QB), all_heads(COL_KB), all_heads(COL_VB), all_heads(COL_ZB),
            pl.BlockSpec((rows_b, dk), lambda b, h: (h, 0)),
            pl.BlockSpec((rows_b, dk), lambda b, h: (h, 0)),
            head_tab, head_tab, head_tab,
            *w_specs,
        ],
        out_specs=[head_cols(0), all_heads(0), *w_specs],
        out_shape=[jax.ShapeDtypeStruct((B, S, D_A), BF16)] * 2
        + [jax.ShapeDtypeStruct(w.shape, BF16) for w in weights],
        scratch_shapes=[
            pltpu.VMEM((HEAD_DIM + VT_PAD, S), BF16),
            pltpu.VMEM((2 * len(MOBA_GROUPS[0]), S, L), F32),
            pltpu.VMEM((N_HEADS, dk, dk), F32),
        ],
        compiler_params=pltpu.CompilerParams(
            dimension_semantics=("parallel", "arbitrary"),
            vmem_limit_bytes=VMEM_LIMIT_V7X),
    )(rel_bias, proj3, proj3, proj3, proj3, tabs,
      chunk_decay, proj3, proj3, proj3, proj3, cos, sin, decay, zeta, xi, *weights)
    return outs[0], outs[1], outs[2:]


def _tail_kernel(ya_ref, yb_ref, ga_ref, gb_ref, x_ref, p_ref, wa_ref, wb_ref, wo_ref, wpg_ref,
                 wpp_ref, gple_ref, gfin_ref, o_ref):
    ua = jnp.dot(ya_ref[...], wa_ref[...], preferred_element_type=F32)
    ub = jnp.dot(yb_ref[...], wb_ref[...], preferred_element_type=F32)
    merged = (jax.nn.sigmoid(ga_ref[...].astype(F32)) * ua
              + jax.nn.sigmoid(gb_ref[...].astype(F32)) * ub)
    x1 = x_ref[...] + jnp.dot(merged.astype(BF16), wo_ref[...], preferred_element_type=F32)
    hp = _rms_norm(x1, gple_ref[...]).astype(BF16)
    gate = jax.nn.sigmoid(jnp.dot(hp, wpg_ref[...], preferred_element_type=F32))
    pe = jnp.dot(p_ref[...].astype(BF16), wpp_ref[...], preferred_element_type=F32)
    x2 = x1 + gate * pe
    o_ref[...] = _rms_norm(x2, gfin_ref[...])


def _tail(ya, yb, proj, x2d, p2d, wa, wb, wo, wpg, wpp, g_ple, g_final, tm):
    m, d = x2d.shape

    def const(shape):
        return pl.BlockSpec(shape, lambda i: (0, 0), pipeline_mode=pl.Buffered(1))

    return pl.pallas_call(
        _tail_kernel,
        name="tail",
        grid=(m // tm,),
        in_specs=[
            pl.BlockSpec((tm, D_A), lambda i: (i, 0)),
            pl.BlockSpec((tm, D_A), lambda i: (i, 0)),
            pl.BlockSpec((tm, d), lambda i: (i, COL_GA)),
            pl.BlockSpec((tm, d), lambda i: (i, COL_GB)),
            pl.BlockSpec((tm, d), lambda i: (i, 0)),
            pl.BlockSpec((tm, D_PLE), lambda i: (i, 0)),
            const((D_A, d)), const((D_A, d)), const((d, d)), const((d, d)), const((D_PLE, d)),
            const((1, d)), const((1, d)),
        ],
        out_specs=pl.BlockSpec((tm, d), lambda i: (i, 0)),
        out_shape=jax.ShapeDtypeStruct((m, d), F32),
        compiler_params=pltpu.CompilerParams(
            dimension_semantics=("parallel",),
            vmem_limit_bytes=VMEM_LIMIT_V7X),
    )(ya, yb, proj, proj, x2d, p2d, wa, wb, wo, wpg, wpp, g_ple, g_final)


def kernel(x, p, g_mix, w_in, w_a, w_b, w_out, g_ple, w_ple_gate, w_ple_proj, rel_bias, g_final):
    B, S, d = x.shape
    assert w_in.shape[0] == 1, "the fused tail applies the final norm: single-layer stacks only"
    tabs = _bias_tables(rel_bias)
    x2d = x.reshape(B * S, d)
    col_scale = jnp.where(jnp.arange(D_IN) < D_A, HEAD_DIM ** -0.5 * LOG2E, 1.0).astype(F32)
    proj = _project(x2d, g_mix, w_in[0], col_scale[None, :], tm=1024, tn=1024)
    proj3 = proj.reshape(B, S, D_IN)
    wpp_view = w_ple_proj[0].reshape(4 * D_PLE, d // 4)
    ya, yb, (wa, wb, wo, wpg, wpp) = _attention(
        proj3, tabs, rel_bias, (w_a[0], w_b[0], w_out[0], w_ple_gate[0], wpp_view))
    out = _tail(ya.reshape(B * S, D_A), yb.reshape(B * S, D_A), proj, x2d,
                p[0].reshape(B * S, D_PLE), wa, wb, wo, wpg, wpp.reshape(D_PLE, d),
                g_ple, g_final[None, :], tm=256)
    return out.reshape(B, S, d)
```

```python
import functools
import math

import numpy as np
import jax
import jax.numpy as jnp
from jax import lax
from jax.experimental import pallas as pl
from jax.experimental.pallas import tpu as pltpu

D_MODEL = 2048
N_HEADS = 8
HEAD_DIM = 128
D_A = N_HEADS * HEAD_DIM
MOBA_BLOCK = 256
MOBA_TOPK = 3
N_BUCKETS = 32
MAX_DISTANCE = 128
RET_CHUNK = 128
ROT_BASE = 10000.0
D_PLE = 256
EPS = 1e-6

COL_QA, COL_KA, COL_VA, COL_ZA = (i * N_HEADS for i in range(4))
COL_QB, COL_KB, COL_VB, COL_ZB = 4, 5, 6, 7

VMEM_LIMIT_V7X = 56 * 1024 * 1024
BF16_SUBLANES = 16
VT_PAD = BF16_SUBLANES
MOBA_GROUPS = ((7, 0), (6, 1), (5, 2), (4, 3))
RET_LANES = 2
X_PARTS = 4
GATE_ROWS = 1024
GATE_COLS = 256
GATE_EVERY = 8
GATE_DELAY = 2

F32 = jnp.float32
BF16 = jnp.bfloat16
NEG_INF = float("-inf")
LOG2E = math.log2(math.e)
NT_DIMS = (((1,), (1,)), ((), ()))
TN_DIMS = (((0,), (0,)), ((), ()))


def _rms_norm(x, g):
    return x * lax.rsqrt(jnp.mean(x * x, axis=-1, keepdims=True) + EPS) * g


def _silu(z):
    return z * jax.nn.sigmoid(z)


def _sweeps(gens):
    gens = list(gens)
    while gens:
        for g in list(gens):
            try:
                next(g)
            except StopIteration:
                gens.remove(g)
        yield


def _round_robin(gens):
    for _ in _sweeps(gens):
        pass


def _chain(gens):
    for g in gens:
        yield from g


def _proj_kernel(*refs):
    x_refs, (g_ref, w_ref, cs_ref, o_ref, h_ref) = refs[:X_PARTS], refs[X_PARTS:]

    @pl.when(pl.program_id(1) == 0)
    def _():
        rows = h_ref.shape[0] // X_PARTS
        for q, x_ref in enumerate(x_refs):
            h_ref[q * rows:(q + 1) * rows, :] = _rms_norm(x_ref[...], g_ref[...]).astype(BF16)

    acc = jnp.dot(h_ref[...], w_ref[...].astype(BF16), preferred_element_type=F32)
    o_ref[...] = (acc * cs_ref[...]).astype(o_ref.dtype)


def _project(x2d, g, w, col_scale, n, tm, tn):
    m, d = x2d.shape
    n_i, n_j = m // tm, n // tn
    assert n_j >= X_PARTS

    def x_part(q):
        def index(i, j):
            nxt = jnp.minimum(i + (j >= n_j - X_PARTS + q).astype(jnp.int32), n_i - 1)
            return (nxt * X_PARTS + q, 0)
        return pl.BlockSpec((tm // X_PARTS, d), index)

    return pl.pallas_call(
        _proj_kernel,
        name="proj",
        grid=(n_i, n_j),
        in_specs=[
            *[x_part(q) for q in range(X_PARTS)],
            pl.BlockSpec((1, d), lambda i, j: (0, 0)),
            pl.BlockSpec((d, tn), lambda i, j: (0, j)),
            pl.BlockSpec((1, tn), lambda i, j: (0, j)),
        ],
        out_specs=[pl.BlockSpec((tm, tn), lambda i, j: (i, j)),
                   pl.BlockSpec((tm, d), lambda i, j: (i, 0))],
        out_shape=[jax.ShapeDtypeStruct((m, n), BF16), jax.ShapeDtypeStruct((m, d), BF16)],
        compiler_params=pltpu.CompilerParams(
            dimension_semantics=("arbitrary", "arbitrary"),
            vmem_limit_bytes=VMEM_LIMIT_V7X),
    )(*[x2d] * X_PARTS, g, w, col_scale)


def _t5_bucket_np(n):
    max_exact = N_BUCKETS // 2
    nf = np.maximum(n, 1).astype(np.float32)
    val = (np.log(nf / max_exact) / np.float32(math.log(MAX_DISTANCE / max_exact))
           * (N_BUCKETS - max_exact))
    large = np.minimum(max_exact + val.astype(np.int32), N_BUCKETS - 1)
    return np.where(n < max_exact, n, large).astype(np.int32)


def _bucket_tables():
    j = np.arange(MOBA_BLOCK)[:, None]
    i = np.arange(MOBA_BLOCK)[None, :]
    own = _t5_bucket_np(np.maximum(i - j, 0))
    own = np.where(j > i, -1, own)
    prev = _t5_bucket_np(MOBA_BLOCK + i - j)
    return np.stack([own, prev]).astype(np.int32)


def _bias_kernel(rb_ref, bkt_ref, o_ref):
    h = pl.program_id(0)
    bkt = bkt_ref[...]
    tab = jnp.where(bkt < 0, NEG_INF, 0.0).astype(F32)
    for b in range(N_BUCKETS):
        tab = jnp.where(bkt == b, rb_ref[b, h] * LOG2E, tab)
    o_ref[...] = tab


def _bias_tables(rel_bias):
    bkt = jnp.asarray(_bucket_tables())
    L = MOBA_BLOCK
    return pl.pallas_call(
        _bias_kernel,
        name="t5_bias",
        grid=(N_HEADS,),
        in_specs=[
            pl.BlockSpec(memory_space=pltpu.SMEM),
            pl.BlockSpec((2, L, L), lambda h: (0, 0, 0)),
        ],
        out_specs=pl.BlockSpec((None, 2, L, L), lambda h: (h, 0, 0, 0)),
        out_shape=jax.ShapeDtypeStruct((N_HEADS, 2, L, L), F32),
    )(rel_bias, bkt)


def _moba_sweeps(rb_ref, q_ref, k_ref, v_ref, z_ref, tab_ref, o_ref, vt_ref, s_ref):
    L = MOBA_BLOCK
    nb = k_ref.shape[0] // L
    h = pl.program_id(1)

    kf = k_ref[...].astype(F32).reshape(nb, L, HEAD_DIM)
    km = jnp.mean(kf, axis=1)
    km_hi = km.astype(BF16)
    km_lo = (km - km_hi.astype(F32)).astype(BF16)
    km2 = jnp.concatenate([km_hi, km_lo], axis=0)
    vt_ref[:HEAD_DIM, :] = v_ref[...].T
    ones_row = lax.broadcasted_iota(jnp.int32, (VT_PAD, vt_ref.shape[1]), 0) == 0
    vt_ref[HEAD_DIM:, :] = jnp.where(ones_row, 1.0, 0.0).astype(BF16)
    far_bias = rb_ref[N_BUCKETS - 1, h] * LOG2E
    row = lax.broadcasted_iota(jnp.int32, (nb, L), 0)

    def scores(qi, slot, out):
        q = q_ref[qi * L:(qi + 1) * L, :]
        sel_add = None
        if qi > MOBA_TOPK:
            g2 = lax.dot_general(km2, q, NT_DIMS, preferred_element_type=F32)
            g = g2[:nb] + g2[nb:]
            cnt = jnp.zeros((nb, L), jnp.int32)
            for mm in range(qi):
                gm = g[mm:mm + 1, :]
                ge = jnp.where(gm >= g, 1, 0)
                gt = jnp.where(gm > g, 1, 0)
                cnt = cnt + jnp.where(row > mm, ge, gt)
            sel_add = jnp.where(cnt < MOBA_TOPK, 0.0, NEG_INF)

        sbuf = s_ref.at[slot]
        shifts = []
        m = None
        for n in range(qi + 1):
            s = lax.dot_general(k_ref[n * L:(n + 1) * L, :], q, NT_DIMS,
                                preferred_element_type=F32)
            shift = None
            if n == qi:
                s = s + tab_ref[0]
            else:
                if n == qi - 1:
                    s = s + tab_ref[1]
                if sel_add is not None:
                    shift = sel_add[n:n + 1, :]
                if n < qi - 1:
                    shift = far_bias if shift is None else shift + far_bias
            sbuf[n * L:(n + 1) * L, :] = s
            tmax = jnp.max(s, axis=0, keepdims=True)
            if shift is not None:
                tmax = tmax + shift
            shifts.append(shift)
            m = tmax if m is None else jnp.maximum(m, tmax)
            yield
        out.append((m, shifts))

    def outputs(qi, slot, m, shifts):
        sbuf = s_ref.at[slot]
        acc = None
        for n in range(qi + 1):
            off = m if shifts[n] is None else m - shifts[n]
            p = jnp.exp2((sbuf[n * L:(n + 1) * L, :] - off).astype(BF16))
            pv = jnp.dot(vt_ref[:, n * L:(n + 1) * L], p,
                         preferred_element_type=F32)
            acc = pv if acc is None else acc + pv
            yield
        l = acc[HEAD_DIM:HEAD_DIM + 1, :]
        y = (acc[:HEAD_DIM, :] * (1.0 / l)).T
        z = z_ref[qi * L:(qi + 1) * L, :].astype(F32)
        o_ref[qi * L:(qi + 1) * L, :] = (y * _silu(z)).astype(o_ref.dtype)

    width = len(MOBA_GROUPS[0])
    pending = []
    for t, grp in enumerate(MOBA_GROUPS):
        outs = [[] for _ in grp]
        slots = [(t % 2) * width + i for i in range(width)]
        yield from _sweeps([scores(qi, slots[i], outs[i]) for i, qi in enumerate(grp)]
                           + [outputs(*args) for args in pending])
        pending = [(qi, slots[i]) + outs[i][0] for i, qi in enumerate(grp)]
    yield from _sweeps([outputs(*args) for args in pending])


def _retention_tables(S):
    C = RET_CHUNK
    dk = HEAD_DIM
    pos = jnp.arange(S, dtype=F32)
    theta = 1.0 / (ROT_BASE ** jnp.linspace(0.0, 1.0, dk // 2, dtype=F32))
    ang = pos[:, None] * theta[None, :]
    cos = jnp.repeat(jnp.cos(ang), 2, axis=1)
    sin = jnp.repeat(jnp.sin(ang), 2, axis=1) * jnp.tile(jnp.array([-1.0, 1.0], F32), dk // 2)
    log_gamma = jnp.log(1.0 - 2.0 ** (-5.0 - jnp.arange(N_HEADS, dtype=F32)))
    i = jnp.arange(C)
    diff = i[:, None] - i[None, :]
    k_scale = dk ** -0.5
    decay = jnp.where(diff >= 0,
                      jnp.exp(jnp.maximum(diff, 0).astype(F32) * log_gamma[:, None, None]),
                      0.0) * k_scale
    zeta = jnp.exp((C - 1 - i).astype(F32)[None, :] * log_gamma[:, None]) * k_scale
    xi = jnp.exp((i + 1).astype(F32)[None, :] * log_gamma[:, None])
    zeta = jnp.broadcast_to(zeta[:, :, None], (N_HEADS, C, dk))
    xi = jnp.broadcast_to(xi[:, :, None], (N_HEADS, C, dk))
    chunk_decay = jnp.exp(C * log_gamma)
    return cos, sin, decay, zeta, xi, chunk_decay


def _ret_chains(cd_ref, q_ref, k_ref, v_ref, z_ref, cos_ref, sin_ref, dm_ref, zeta_ref, xi_ref,
                o_ref, r_ref, first):
    C = RET_CHUNK
    n_chunks = q_ref.shape[0] // C

    @pl.when(first)
    def _():
        r_ref[...] = jnp.zeros_like(r_ref)

    lane = lax.broadcasted_iota(jnp.int32, (C, HEAD_DIM), 1)
    even = (lane % 2) == 0

    def rot(t, cosv, sinv):
        partner = jnp.where(even, pltpu.roll(t, HEAD_DIM - 1, 1), pltpu.roll(t, 1, 1))
        return t * cosv + partner * sinv

    def head(h):
        cols = slice(h * HEAD_DIM, (h + 1) * HEAD_DIM)
        for c in range(n_chunks):
            rows = slice(c * C, (c + 1) * C)
            cosv = cos_ref[rows, :]
            sinv = sin_ref[rows, :]
            q = rot(q_ref[rows, cols].astype(F32), cosv, sinv)
            k = rot(k_ref[rows, cols].astype(F32), cosv, sinv)
            v = v_ref[rows, cols]
            qb = q.astype(BF16)
            yield
            inner = lax.dot_general(qb, k.astype(BF16), NT_DIMS,
                                    preferred_element_type=F32) * dm_ref[h]
            r_old = r_ref[h]
            cross = xi_ref[h] * jnp.dot(qb, r_old.astype(BF16), preferred_element_type=F32)
            kz = (k * zeta_ref[h]).astype(BF16)
            r_ref[h] = r_old * cd_ref[h] + lax.dot_general(kz, v, TN_DIMS,
                                                           preferred_element_type=F32)
            yield
            o = jnp.dot(inner.astype(BF16), v, preferred_element_type=F32) + cross
            yield
            o = o * lax.rsqrt(jnp.mean(o * o, axis=-1, keepdims=True) + EPS)
            o_ref[rows, cols] = (o * _silu(z_ref[rows, cols].astype(F32))).astype(o_ref.dtype)
            yield

    return [head(h) for h in range(N_HEADS)]


def _attn_kernel(n_cast, rb_ref, qa_ref, ka_ref, va_ref, za_ref, tab_ref,
                 cd_ref, qb_ref, kb_ref, vb_ref, zb_ref, cos_ref, sin_ref, dm_ref, zeta_ref, xi_ref,
                 hx_ref, wg_ref, *refs):
    w32_refs = refs[:n_cast]
    ya_ref, yb_ref, gate_ref = refs[n_cast:n_cast + 3]
    w16_refs = refs[n_cast + 3:2 * n_cast + 3]
    vt_ref, s_ref, r_ref = refs[2 * n_cast + 3:]

    def gate_proj():
        tm, tn = gate_ref.shape
        for _ in range(GATE_DELAY):
            yield
        for r0 in range(0, tm, GATE_ROWS):
            for c0 in range(0, tn, GATE_COLS):
                acc = jnp.dot(hx_ref[r0:r0 + GATE_ROWS, :],
                              wg_ref[:, c0:c0 + GATE_COLS].astype(BF16),
                              preferred_element_type=F32)
                gate_ref[r0:r0 + GATE_ROWS, c0:c0 + GATE_COLS] = acc.astype(gate_ref.dtype)
                for _ in range(GATE_EVERY):
                    yield

    def casts():
        for w32, w16 in zip(w32_refs, w16_refs):
            w16[...] = w32[...].astype(BF16)
            yield

    heads = _ret_chains(cd_ref, qb_ref, kb_ref, vb_ref, zb_ref, cos_ref, sin_ref, dm_ref, zeta_ref,
                        xi_ref, yb_ref, r_ref, first=pl.program_id(1) == 0)
    lanes = [_chain(heads[i::RET_LANES]) for i in range(RET_LANES)]
    moba = _moba_sweeps(rb_ref, qa_ref, ka_ref, va_ref, za_ref, tab_ref, ya_ref, vt_ref, s_ref)
    _round_robin([moba] + lanes + [casts(), gate_proj()])


def _attention(proj3, tabs, rel_bias, weights, hx, w_in, gate_col0, tm, tn):
    B, S, _ = proj3.shape
    L = MOBA_BLOCK
    C = RET_CHUNK
    dk = HEAD_DIM
    rows_b = S // N_HEADS
    assert sorted(qi for grp in MOBA_GROUPS for qi in grp) == list(range(S // L))
    assert rows_b % C == 0
    n_steps = B * N_HEADS
    gate_tiles = 2 * D_MODEL // tn
    assert (B * S // tm) * gate_tiles == n_steps
    cos, sin, decay, zeta, xi, chunk_decay = _retention_tables(S)

    def gate_tile(b, h):
        step = b * N_HEADS + h
        return step // gate_tiles, step % gate_tiles

    def head_cols(col0):
        return pl.BlockSpec((None, S, HEAD_DIM), lambda b, h: (b, 0, col0 + h))

    def all_heads(col0):
        return pl.BlockSpec((None, rows_b, D_A), lambda b, h: (b, h, col0))

    def w_slice(w):
        rows = max(w.shape[0] // n_steps, BF16_SUBLANES)
        n_slices = w.shape[0] // rows
        assert rows * n_slices == w.shape[0] and n_slices <= n_steps, w.shape
        return pl.BlockSpec((rows, w.shape[1]),
                            lambda b, h: (jnp.minimum(b * N_HEADS + h, n_slices - 1), 0))

    head_tab = pl.BlockSpec((N_HEADS, C, dk), lambda b, h: (0, 0, 0))
    w_specs = [w_slice(w) for w in weights]
    smem = pl.BlockSpec(memory_space=pltpu.SMEM)
    outs = pl.pallas_call(
        functools.partial(_attn_kernel, len(weights)),
        name="attention",
        grid=(B, N_HEADS),
        in_specs=[
            smem,
            head_cols(COL_QA), head_cols(COL_KA), head_cols(COL_VA), head_cols(COL_ZA),
            pl.BlockSpec((None, 2, L, L), lambda b, h: (h, 0, 0, 0)),
            smem,
            all_heads(COL_QB), all_heads(COL_KB), all_heads(COL_VB), all_heads(COL_ZB),
            pl.BlockSpec((rows_b, dk), lambda b, h: (h, 0)),
            pl.BlockSpec((rows_b, dk), lambda b, h: (h, 0)),
            head_tab, head_tab, head_tab,
            pl.BlockSpec((tm, hx.shape[1]), lambda b, h: (gate_tile(b, h)[0], 0)),
            pl.BlockSpec((w_in.shape[0], tn),
                         lambda b, h: (0, gate_col0 // tn + gate_tile(b, h)[1])),
            *w_specs,
        ],
        out_specs=[head_cols(0), all_heads(0), pl.BlockSpec((tm, tn), gate_tile), *w_specs],
        out_shape=[jax.ShapeDtypeStruct((B, S, D_A), BF16)] * 2
        + [jax.ShapeDtypeStruct((B * S, 2 * D_MODEL), BF16)]
        + [jax.ShapeDtypeStruct(w.shape, BF16) for w in weights],
        scratch_shapes=[
            pltpu.VMEM((HEAD_DIM + VT_PAD, S), BF16),
            pltpu.VMEM((2 * len(MOBA_GROUPS[0]), S, L), F32),
            pltpu.VMEM((N_HEADS, dk, dk), F32),
        ],
        compiler_params=pltpu.CompilerParams(
            dimension_semantics=("arbitrary", "arbitrary"),
            vmem_limit_bytes=VMEM_LIMIT_V7X),
    )(rel_bias, proj3, proj3, proj3, proj3, tabs,
      chunk_decay, proj3, proj3, proj3, proj3, cos, sin, decay, zeta, xi, hx, w_in, *weights)
    return outs[0], outs[1], outs[2], outs[3:]


def _tail_kernel(ya_ref, yb_ref, ga_ref, gb_ref, x_ref, p_ref, wa_ref, wb_ref, wo_ref, wpg_ref,
                 wpp_ref, gple_ref, gfin_ref, o_ref):
    ua = jnp.dot(ya_ref[...], wa_ref[...], preferred_element_type=F32)
    ub = jnp.dot(yb_ref[...], wb_ref[...], preferred_element_type=F32)
    merged = (jax.nn.sigmoid(ga_ref[...].astype(F32)) * ua
              + jax.nn.sigmoid(gb_ref[...].astype(F32)) * ub)
    x1 = x_ref[...] + jnp.dot(merged.astype(BF16), wo_ref[...], preferred_element_type=F32)
    hp = _rms_norm(x1, gple_ref[...]).astype(BF16)
    gate = jax.nn.sigmoid(jnp.dot(hp, wpg_ref[...], preferred_element_type=F32))
    pe = jnp.dot(p_ref[...].astype(BF16), wpp_ref[...], preferred_element_type=F32)
    x2 = x1 + gate * pe
    o_ref[...] = _rms_norm(x2, gfin_ref[...])


def _tail(ya, yb, gates, x2d, p2d, wa, wb, wo, wpg, wpp, g_ple, g_final, tm):
    m, d = x2d.shape

    def const(shape):
        return pl.BlockSpec(shape, lambda i: (0, 0), pipeline_mode=pl.Buffered(1))

    return pl.pallas_call(
        _tail_kernel,
        name="tail",
        grid=(m // tm,),
        in_specs=[
            pl.BlockSpec((tm, D_A), lambda i: (i, 0)),
            pl.BlockSpec((tm, D_A), lambda i: (i, 0)),
            pl.BlockSpec((tm, d), lambda i: (i, 0)),
            pl.BlockSpec((tm, d), lambda i: (i, 1)),
            pl.BlockSpec((tm, d), lambda i: (i, 0)),
            pl.BlockSpec((tm, D_PLE), lambda i: (i, 0)),
            const((D_A, d)), const((D_A, d)), const((d, d)), const((d, d)), const((D_PLE, d)),
            const((1, d)), const((1, d)),
        ],
        out_specs=pl.BlockSpec((tm, d), lambda i: (i, 0)),
        out_shape=jax.ShapeDtypeStruct((m, d), F32),
        compiler_params=pltpu.CompilerParams(
            dimension_semantics=("parallel",),
            vmem_limit_bytes=VMEM_LIMIT_V7X),
    )(ya, yb, gates, gates, x2d, p2d, wa, wb, wo, wpg, wpp, g_ple, g_final)


def kernel(x, p, g_mix, w_in, w_a, w_b, w_out, g_ple, w_ple_gate, w_ple_proj, rel_bias, g_final):
    B, S, d = x.shape
    assert w_in.shape[0] == 1, "the fused tail applies the final norm: single-layer stacks only"
    tabs = _bias_tables(rel_bias)
    x2d = x.reshape(B * S, d)
    n_mix = 8 * D_A
    col_scale = jnp.where(jnp.arange(n_mix) < D_A, HEAD_DIM ** -0.5 * LOG2E, 1.0).astype(F32)
    proj, hx = _project(x2d, g_mix, w_in[0], col_scale[None, :], n_mix, tm=1024, tn=1024)
    proj3 = proj.reshape(B, S, n_mix)
    ya, yb, gates, (wa, wb, wo, wpg, wpp) = _attention(
        proj3, tabs, rel_bias, (w_a[0], w_b[0], w_out[0], w_ple_gate[0], w_ple_proj[0]),
        hx, w_in[0], n_mix, tm=1024, tn=1024)
    out = _tail(ya.reshape(B * S, D_A), yb.reshape(B * S, D_A), gates, x2d,
                p[0].reshape(B * S, D_PLE), wa, wb, wo, wpg, wpp,
                g_ple, g_final[None, :], tm=256)
    return out.reshape(B, S, d)
```

```python
import functools
import math

import numpy as np
import jax
import jax.numpy as jnp
from jax import lax
from jax.experimental import pallas as pl
from jax.experimental.pallas import tpu as pltpu

D_MODEL = 2048
N_HEADS = 8
HEAD_DIM = 128
D_A = N_HEADS * HEAD_DIM
MOBA_BLOCK = 256
MOBA_TOPK = 3
N_BUCKETS = 32
MAX_DISTANCE = 128
RET_CHUNK = 128
ROT_BASE = 10000.0
D_PLE = 256
EPS = 1e-6

COL_QA, COL_KA, COL_VA, COL_ZA = (i * N_HEADS for i in range(4))
COL_MIXB = 1

VMEM_LIMIT_V7X = 56 * 1024 * 1024
BF16_SUBLANES = 16
VT_PAD = BF16_SUBLANES
MOBA_GROUPS = ((7, 0), (6, 1), (5, 2), (4, 3))
RET_LANES = 2
X_PARTS = 4
GATE_ROWS = 1024
GATE_COLS = 256
GATE_EVERY = 8
GATE_DELAY = 2

F32 = jnp.float32
BF16 = jnp.bfloat16
NEG_INF = float("-inf")
LOG2E = math.log2(math.e)
NT_DIMS = (((1,), (1,)), ((), ()))
TN_DIMS = (((0,), (0,)), ((), ()))


def _rms_norm(x, g):
    return x * lax.rsqrt(jnp.mean(x * x, axis=-1, keepdims=True) + EPS) * g


def _silu(z):
    return z * jax.nn.sigmoid(z)


def _sweeps(gens):
    gens = list(gens)
    while gens:
        for g in list(gens):
            try:
                next(g)
            except StopIteration:
                gens.remove(g)
        yield


def _round_robin(gens):
    for _ in _sweeps(gens):
        pass


def _chain(gens):
    for g in gens:
        yield from g


def _proj_kernel(*refs):
    x_refs, (g_ref, w_ref, cs_ref, o_ref, h_ref) = refs[:X_PARTS], refs[X_PARTS:]

    @pl.when(pl.program_id(1) == 0)
    def _():
        rows = h_ref.shape[0] // X_PARTS
        for q, x_ref in enumerate(x_refs):
            h_ref[q * rows:(q + 1) * rows, :] = _rms_norm(x_ref[...], g_ref[...]).astype(BF16)

    acc = jnp.dot(h_ref[...], w_ref[...].astype(BF16), preferred_element_type=F32)
    o_ref[...] = (acc * cs_ref[...]).astype(o_ref.dtype)


def _project(x2d, g, w, col_scale, n, tm, tn):
    m, d = x2d.shape
    n_i, n_j = m // tm, n // tn
    assert n_j >= X_PARTS

    def x_part(q):
        def index(i, j):
            nxt = jnp.minimum(i + (j >= n_j - X_PARTS + q).astype(jnp.int32), n_i - 1)
            return (nxt * X_PARTS + q, 0)
        return pl.BlockSpec((tm // X_PARTS, d), index)

    return pl.pallas_call(
        _proj_kernel,
        name="proj",
        grid=(n_i, n_j),
        in_specs=[
            *[x_part(q) for q in range(X_PARTS)],
            pl.BlockSpec((1, d), lambda i, j: (0, 0)),
            pl.BlockSpec((d, tn), lambda i, j: (0, j)),
            pl.BlockSpec((1, tn), lambda i, j: (0, j)),
        ],
        out_specs=[pl.BlockSpec((tm, tn), lambda i, j: (i, j)),
                   pl.BlockSpec((tm, d), lambda i, j: (i, 0))],
        out_shape=[jax.ShapeDtypeStruct((m, n), BF16), jax.ShapeDtypeStruct((m, d), BF16)],
        compiler_params=pltpu.CompilerParams(
            dimension_semantics=("arbitrary", "arbitrary"),
            vmem_limit_bytes=VMEM_LIMIT_V7X),
    )(*[x2d] * X_PARTS, g, w, col_scale)


def _t5_bucket_np(n):
    max_exact = N_BUCKETS // 2
    nf = np.maximum(n, 1).astype(np.float32)
    val = (np.log(nf / max_exact) / np.float32(math.log(MAX_DISTANCE / max_exact))
           * (N_BUCKETS - max_exact))
    large = np.minimum(max_exact + val.astype(np.int32), N_BUCKETS - 1)
    return np.where(n < max_exact, n, large).astype(np.int32)


def _bucket_row():
    return _t5_bucket_np(np.arange(2 * MOBA_BLOCK))[None, :].astype(np.int32)


def _bias_kernel(rb_ref, bkt_ref, o_ref):
    h = pl.program_id(0)
    L = MOBA_BLOCK
    bkt = bkt_ref[...]
    f = jnp.zeros(bkt.shape, F32)
    for b in range(N_BUCKETS):
        f = jnp.where(bkt == b, rb_ref[b, h] * LOG2E, f)
    both = pltpu.roll(jnp.broadcast_to(f, (L, 2 * L)), 0, 1, stride=1, stride_axis=0)
    key = lax.broadcasted_iota(jnp.int32, (L, L), 0)
    qry = lax.broadcasted_iota(jnp.int32, (L, L), 1)
    o_ref[0] = jnp.where(qry >= key, both[:, :L], NEG_INF)
    o_ref[1] = both[:, L:]


def _bias_tables(rel_bias):
    bkt = jnp.asarray(_bucket_row())
    L = MOBA_BLOCK
    return pl.pallas_call(
        _bias_kernel,
        name="t5_bias",
        grid=(N_HEADS,),
        in_specs=[
            pl.BlockSpec(memory_space=pltpu.SMEM),
            pl.BlockSpec((1, 2 * L), lambda h: (0, 0)),
        ],
        out_specs=pl.BlockSpec((None, 2, L, L), lambda h: (h, 0, 0, 0)),
        out_shape=jax.ShapeDtypeStruct((N_HEADS, 2, L, L), F32),
    )(rel_bias, bkt)


def _moba_sweeps(rb_ref, q_ref, k_ref, v_ref, z_ref, tab_ref, o_ref, vt_ref, s_ref):
    L = MOBA_BLOCK
    nb = k_ref.shape[0] // L
    h = pl.program_id(1)

    kf = k_ref[...].astype(F32).reshape(nb, L, HEAD_DIM)
    km = jnp.mean(kf, axis=1)
    km_hi = km.astype(BF16)
    km_lo = (km - km_hi.astype(F32)).astype(BF16)
    km2 = jnp.concatenate([km_hi, km_lo], axis=0)
    vt_ref[:HEAD_DIM, :] = v_ref[...].T
    ones_row = lax.broadcasted_iota(jnp.int32, (VT_PAD, vt_ref.shape[1]), 0) == 0
    vt_ref[HEAD_DIM:, :] = jnp.where(ones_row, 1.0, 0.0).astype(BF16)
    far_bias = rb_ref[N_BUCKETS - 1, h] * LOG2E
    row = lax.broadcasted_iota(jnp.int32, (nb, L), 0)

    def scores(qi, slot, out):
        q = q_ref[qi * L:(qi + 1) * L, :]
        sel_add = None
        if qi > MOBA_TOPK:
            g2 = lax.dot_general(km2, q, NT_DIMS, preferred_element_type=F32)
            g = g2[:nb] + g2[nb:]
            cnt = jnp.zeros((nb, L), jnp.int32)
            for mm in range(qi):
                gm = g[mm:mm + 1, :]
                ge = jnp.where(gm >= g, 1, 0)
                gt = jnp.where(gm > g, 1, 0)
                cnt = cnt + jnp.where(row > mm, ge, gt)
            sel_add = jnp.where(cnt < MOBA_TOPK, 0.0, NEG_INF)

        sbuf = s_ref.at[slot]
        shifts = []
        m = None
        for n in range(qi + 1):
            s = lax.dot_general(k_ref[n * L:(n + 1) * L, :], q, NT_DIMS,
                                preferred_element_type=F32)
            shift = None
            if n == qi:
                s = s + tab_ref[0]
            else:
                if n == qi - 1:
                    s = s + tab_ref[1]
                if sel_add is not None:
                    shift = sel_add[n:n + 1, :]
                if n < qi - 1:
                    shift = far_bias if shift is None else shift + far_bias
            sbuf[n * L:(n + 1) * L, :] = s
            tmax = jnp.max(s, axis=0, keepdims=True)
            if shift is not None:
                tmax = tmax + shift
            shifts.append(shift)
            m = tmax if m is None else jnp.maximum(m, tmax)
            yield
        out.append((m, shifts))

    def outputs(qi, slot, m, shifts):
        sbuf = s_ref.at[slot]
        acc = None
        for n in range(qi + 1):
            off = m if shifts[n] is None else m - shifts[n]
            p = jnp.exp2((sbuf[n * L:(n + 1) * L, :] - off).astype(BF16))
            pv = jnp.dot(vt_ref[:, n * L:(n + 1) * L], p,
                         preferred_element_type=F32)
            acc = pv if acc is None else acc + pv
            yield
        l = acc[HEAD_DIM:HEAD_DIM + 1, :]
        y = (acc[:HEAD_DIM, :] * (1.0 / l)).T
        z = z_ref[qi * L:(qi + 1) * L, :].astype(F32)
        o_ref[qi * L:(qi + 1) * L, :] = (y * _silu(z)).astype(o_ref.dtype)

    width = len(MOBA_GROUPS[0])
    pending = []
    for t, grp in enumerate(MOBA_GROUPS):
        outs = [[] for _ in grp]
        slots = [(t % 2) * width + i for i in range(width)]
        yield from _sweeps([scores(qi, slots[i], outs[i]) for i, qi in enumerate(grp)]
                           + [outputs(*args) for args in pending])
        pending = [(qi, slots[i]) + outs[i][0] for i, qi in enumerate(grp)]
    yield from _sweeps([outputs(*args) for args in pending])


def _retention_tables(S):
    C = RET_CHUNK
    dk = HEAD_DIM
    pos = jnp.arange(S, dtype=F32)
    theta = 1.0 / (ROT_BASE ** jnp.linspace(0.0, 1.0, dk // 2, dtype=F32))
    ang = pos[:, None] * theta[None, :]
    cos = jnp.repeat(jnp.cos(ang), 2, axis=1)
    sin = jnp.repeat(jnp.sin(ang), 2, axis=1) * jnp.tile(jnp.array([-1.0, 1.0], F32), dk // 2)
    log_gamma = jnp.log(1.0 - 2.0 ** (-5.0 - jnp.arange(N_HEADS, dtype=F32)))
    i = jnp.arange(C)
    diff = i[:, None] - i[None, :]
    k_scale = dk ** -0.5
    decay = jnp.where(diff >= 0,
                      jnp.exp(jnp.maximum(diff, 0).astype(F32) * log_gamma[:, None, None]),
                      0.0) * k_scale
    zeta = jnp.exp((C - 1 - i).astype(F32)[None, :] * log_gamma[:, None]) * k_scale
    xi = jnp.exp((i + 1).astype(F32)[None, :] * log_gamma[:, None])
    zeta = jnp.broadcast_to(zeta[:, :, None], (N_HEADS, C, dk))
    xi = jnp.broadcast_to(xi[:, :, None], (N_HEADS, C, dk))
    chunk_decay = jnp.exp(C * log_gamma)
    return cos, sin, decay, zeta, xi, chunk_decay


def _ret_chains(cd_ref, q_ref, k_ref, v_ref, z_ref, cos_ref, sin_ref, dm_ref, zeta_ref, xi_ref,
                o_ref, r_ref, first):
    C = RET_CHUNK
    n_chunks = q_ref.shape[0] // C

    @pl.when(first)
    def _():
        r_ref[...] = jnp.zeros_like(r_ref)

    lane = lax.broadcasted_iota(jnp.int32, (C, HEAD_DIM), 1)
    even = (lane % 2) == 0

    def rot(t, cosv, sinv):
        partner = jnp.where(even, pltpu.roll(t, HEAD_DIM - 1, 1), pltpu.roll(t, 1, 1))
        return t * cosv + partner * sinv

    def head(h):
        cols = slice(h * HEAD_DIM, (h + 1) * HEAD_DIM)
        for c in range(n_chunks):
            rows = slice(c * C, (c + 1) * C)
            cosv = cos_ref[rows, :]
            sinv = sin_ref[rows, :]
            q = rot(q_ref[rows, cols].astype(F32), cosv, sinv)
            k = rot(k_ref[rows, cols].astype(F32), cosv, sinv)
            v = v_ref[rows, cols]
            qb = q.astype(BF16)
            yield
            inner = lax.dot_general(qb, k.astype(BF16), NT_DIMS,
                                    preferred_element_type=F32) * dm_ref[h]
            r_old = r_ref[h]
            cross = xi_ref[h] * jnp.dot(qb, r_old.astype(BF16), preferred_element_type=F32)
            kz = (k * zeta_ref[h]).astype(BF16)
            r_ref[h] = r_old * cd_ref[h] + lax.dot_general(kz, v, TN_DIMS,
                                                           preferred_element_type=F32)
            yield
            o = jnp.dot(inner.astype(BF16), v, preferred_element_type=F32) + cross
            yield
            o = o * lax.rsqrt(jnp.mean(o * o, axis=-1, keepdims=True) + EPS)
            o_ref[rows, cols] = (o * _silu(z_ref[rows, cols].astype(F32))).astype(o_ref.dtype)
            yield

    return [head(h) for h in range(N_HEADS)]


def _attn_kernel(n_cast, rb_ref, qa_ref, ka_ref, va_ref, za_ref, tab_ref,
                 cd_ref, mixb_ref, rope_ref, dm_ref, zeta_ref, xi_ref, hx_ref, wg_ref, *refs):
    w32_refs = refs[:n_cast]
    ya_ref, yb_ref, gate_ref = refs[n_cast:n_cast + 3]
    w16_refs = refs[n_cast + 3:2 * n_cast + 3]
    vt_ref, s_ref, r_ref = refs[2 * n_cast + 3:]

    def gate_proj():
        tm, tn = gate_ref.shape
        for _ in range(GATE_DELAY):
            yield
        for r0 in range(0, tm, GATE_ROWS):
            for c0 in range(0, tn, GATE_COLS):
                acc = jnp.dot(hx_ref[r0:r0 + GATE_ROWS, :],
                              wg_ref[:, c0:c0 + GATE_COLS].astype(BF16),
                              preferred_element_type=F32)
                gate_ref[r0:r0 + GATE_ROWS, c0:c0 + GATE_COLS] = acc.astype(gate_ref.dtype)
                for _ in range(GATE_EVERY):
                    yield

    def casts():
        for w32, w16 in zip(w32_refs, w16_refs):
            w16[...] = w32[...].astype(BF16)
            yield

    qb_ref, kb_ref, vb_ref, zb_ref = (mixb_ref.at[:, c * D_A:(c + 1) * D_A] for c in range(4))
    cos_ref, sin_ref = (rope_ref.at[:, c * HEAD_DIM:(c + 1) * HEAD_DIM] for c in range(2))
    heads = _ret_chains(cd_ref, qb_ref, kb_ref, vb_ref, zb_ref, cos_ref, sin_ref, dm_ref, zeta_ref,
                        xi_ref, yb_ref, r_ref, first=pl.program_id(1) == 0)
    lanes = [_chain(heads[i::RET_LANES]) for i in range(RET_LANES)]
    moba = _moba_sweeps(rb_ref, qa_ref, ka_ref, va_ref, za_ref, tab_ref, ya_ref, vt_ref, s_ref)
    _round_robin([moba] + lanes + [casts(), gate_proj()])


def _attention(proj3, tabs, rel_bias, weights, hx, w_in, gate_col0, tm, tn):
    B, S, _ = proj3.shape
    L = MOBA_BLOCK
    C = RET_CHUNK
    dk = HEAD_DIM
    rows_b = S // N_HEADS
    assert sorted(qi for grp in MOBA_GROUPS for qi in grp) == list(range(S // L))
    assert rows_b % C == 0
    n_steps = B * N_HEADS
    gate_tiles = 2 * D_MODEL // tn
    assert (B * S // tm) * gate_tiles == n_steps
    cos, sin, decay, zeta, xi, chunk_decay = _retention_tables(S)

    def gate_tile(b, h):
        step = b * N_HEADS + h
        return step // gate_tiles, step % gate_tiles

    def head_cols(col0):
        return pl.BlockSpec((None, S, HEAD_DIM), lambda b, h: (b, 0, col0 + h))

    def all_heads(col0):
        return pl.BlockSpec((None, rows_b, D_A), lambda b, h: (b, h, col0))

    def w_slice(w):
        rows = max(w.shape[0] // n_steps, BF16_SUBLANES)
        n_slices = w.shape[0] // rows
        assert rows * n_slices == w.shape[0] and n_slices <= n_steps, w.shape
        return pl.BlockSpec((rows, w.shape[1]),
                            lambda b, h: (jnp.minimum(b * N_HEADS + h, n_slices - 1), 0))

    head_tab = pl.BlockSpec((N_HEADS, C, dk), lambda b, h: (0, 0, 0))
    w_specs = [w_slice(w) for w in weights]
    smem = pl.BlockSpec(memory_space=pltpu.SMEM)
    outs = pl.pallas_call(
        functools.partial(_attn_kernel, len(weights)),
        name="attention",
        grid=(B, N_HEADS),
        in_specs=[
            smem,
            head_cols(COL_QA), head_cols(COL_KA), head_cols(COL_VA), head_cols(COL_ZA),
            pl.BlockSpec((None, 2, L, L), lambda b, h: (h, 0, 0, 0)),
            smem,
            pl.BlockSpec((None, rows_b, 4 * D_A), lambda b, h: (b, h, COL_MIXB)),
            pl.BlockSpec((rows_b, 2 * dk), lambda b, h: (h, 0)),
            head_tab, head_tab, head_tab,
            pl.BlockSpec((tm, hx.shape[1]), lambda b, h: (gate_tile(b, h)[0], 0)),
            pl.BlockSpec((w_in.shape[0], tn),
                         lambda b, h: (0, gate_col0 // tn + gate_tile(b, h)[1])),
            *w_specs,
        ],
        out_specs=[head_cols(0), all_heads(0), pl.BlockSpec((tm, tn), gate_tile), *w_specs],
        out_shape=[jax.ShapeDtypeStruct((B, S, D_A), BF16)] * 2
        + [jax.ShapeDtypeStruct((B * S, 2 * D_MODEL), BF16)]
        + [jax.ShapeDtypeStruct(w.shape, BF16) for w in weights],
        scratch_shapes=[
            pltpu.VMEM((HEAD_DIM + VT_PAD, S), BF16),
            pltpu.VMEM((2 * len(MOBA_GROUPS[0]), S, L), F32),
            pltpu.VMEM((N_HEADS, dk, dk), F32),
        ],
        compiler_params=pltpu.CompilerParams(
            dimension_semantics=("arbitrary", "arbitrary"),
            vmem_limit_bytes=VMEM_LIMIT_V7X),
    )(rel_bias, proj3, proj3, proj3, proj3, tabs,
      chunk_decay, proj3, jnp.concatenate([cos, sin], axis=1), decay, zeta, xi, hx, w_in, *weights)
    return outs[0], outs[1], outs[2], outs[3:]


def _tail_kernel(ya_ref, yb_ref, ga_ref, gb_ref, x_ref, p_ref, wa_ref, wb_ref, wo_ref, wpg_ref,
                 wpp_ref, gple_ref, gfin_ref, o_ref):
    ua = jnp.dot(ya_ref[...], wa_ref[...], preferred_element_type=F32)
    ub = jnp.dot(yb_ref[...], wb_ref[...], preferred_element_type=F32)
    merged = (jax.nn.sigmoid(ga_ref[...].astype(F32)) * ua
              + jax.nn.sigmoid(gb_ref[...].astype(F32)) * ub)
    x1 = x_ref[...] + jnp.dot(merged.astype(BF16), wo_ref[...], preferred_element_type=F32)
    hp = _rms_norm(x1, gple_ref[...]).astype(BF16)
    gate = jax.nn.sigmoid(jnp.dot(hp, wpg_ref[...], preferred_element_type=F32))
    pe = jnp.dot(p_ref[...].astype(BF16), wpp_ref[...], preferred_element_type=F32)
    x2 = x1 + gate * pe
    o_ref[...] = _rms_norm(x2, gfin_ref[...])


def _tail(ya, yb, gates, x2d, p2d, wa, wb, wo, wpg, wpp, g_ple, g_final, tm):
    m, d = x2d.shape

    def const(shape):
        return pl.BlockSpec(shape, lambda i: (0, 0), pipeline_mode=pl.Buffered(1))

    return pl.pallas_call(
        _tail_kernel,
        name="tail",
        grid=(m // tm,),
        in_specs=[
            pl.BlockSpec((tm, D_A), lambda i: (i, 0)),
            pl.BlockSpec((tm, D_A), lambda i: (i, 0)),
            pl.BlockSpec((tm, d), lambda i: (i, 0)),
            pl.BlockSpec((tm, d), lambda i: (i, 1)),
            pl.BlockSpec((tm, d), lambda i: (i, 0)),
            pl.BlockSpec((tm, D_PLE), lambda i: (i, 0)),
            const((D_A, d)), const((D_A, d)), const((d, d)), const((d, d)), const((D_PLE, d)),
            const((1, d)), const((1, d)),
        ],
        out_specs=pl.BlockSpec((tm, d), lambda i: (i, 0)),
        out_shape=jax.ShapeDtypeStruct((m, d), F32),
        compiler_params=pltpu.CompilerParams(
            dimension_semantics=("parallel",),
            vmem_limit_bytes=VMEM_LIMIT_V7X),
    )(ya, yb, gates, gates, x2d, p2d, wa, wb, wo, wpg, wpp, g_ple, g_final)


def kernel(x, p, g_mix, w_in, w_a, w_b, w_out, g_ple, w_ple_gate, w_ple_proj, rel_bias, g_final):
    B, S, d = x.shape
    assert w_in.shape[0] == 1, "the fused tail applies the final norm: single-layer stacks only"
    tabs = _bias_tables(rel_bias)
    x2d = x.reshape(B * S, d)
    n_mix = 8 * D_A
    col_scale = jnp.where(jnp.arange(n_mix) < D_A, HEAD_DIM ** -0.5 * LOG2E, 1.0).astype(F32)
    proj, hx = _project(x2d, g_mix, w_in[0], col_scale[None, :], n_mix, tm=1024, tn=1024)
    proj3 = proj.reshape(B, S, n_mix)
    ya, yb, gates, (wa, wb, wo, wpg, wpp) = _attention(
        proj3, tabs, rel_bias, (w_a[0], w_b[0], w_out[0], w_ple_gate[0], w_ple_proj[0]),
        hx, w_in[0], n_mix, tm=1024, tn=1024)
    out = _tail(ya.reshape(B * S, D_A), yb.reshape(B * S, D_A), gates, x2d,
                p[0].reshape(B * S, D_PLE), wa, wb, wo, wpg, wpp,
                g_ple, g_final[None, :], tm=256)
    return out.reshape(B, S, d)
```

```python
import functools
import math

import numpy as np
import jax
import jax.numpy as jnp
from jax import lax
from jax.experimental import pallas as pl
from jax.experimental.pallas import tpu as pltpu

D_MODEL = 2048
N_HEADS = 8
HEAD_DIM = 128
D_A = N_HEADS * HEAD_DIM
MOBA_BLOCK = 256
MOBA_TOPK = 3
N_BUCKETS = 32
MAX_DISTANCE = 128
RET_CHUNK = 128
ROT_BASE = 10000.0
D_PLE = 256
EPS = 1e-6

COL_QA, COL_KA, COL_VA, COL_ZA = (i * N_HEADS for i in range(4))
COL_MIXB = 1

VMEM_LIMIT_V7X = 56 * 1024 * 1024
BF16_SUBLANES = 16
VT_PAD = BF16_SUBLANES
MOBA_GROUPS = ((7, 0), (6, 1), (5, 2), (4, 3))
RET_LANES = 2
X_PARTS = 4
GATE_ROWS = 1024
GATE_COLS = 256
GATE_EVERY = 8
GATE_DELAY = 2

F32 = jnp.float32
BF16 = jnp.bfloat16
NEG_INF = float("-inf")
LOG2E = math.log2(math.e)
NT_DIMS = (((1,), (1,)), ((), ()))
TN_DIMS = (((0,), (0,)), ((), ()))


def _rms_norm(x, g):
    return x * lax.rsqrt(jnp.mean(x * x, axis=-1, keepdims=True) + EPS) * g


def _silu(z):
    return z * jax.nn.sigmoid(z)


def _sweeps(gens):
    gens = list(gens)
    while gens:
        for g in list(gens):
            try:
                next(g)
            except StopIteration:
                gens.remove(g)
        yield


def _round_robin(gens):
    for _ in _sweeps(gens):
        pass


def _chain(gens):
    for g in gens:
        yield from g


def _proj_kernel(*refs):
    x_refs, (g_ref, w_ref, cs_ref, o_ref, h_ref) = refs[:X_PARTS], refs[X_PARTS:]

    rows = h_ref.shape[0] // X_PARTS

    @pl.when(pl.program_id(1) == 0)
    def _():
        w = w_ref[...].astype(BF16)
        for q, x_ref in enumerate(x_refs):
            part = slice(q * rows, (q + 1) * rows)
            hq = _rms_norm(x_ref[...], g_ref[...]).astype(BF16)
            h_ref[part, :] = hq
            acc = jnp.dot(hq, w, preferred_element_type=F32)
            o_ref[part, :] = (acc * cs_ref[...]).astype(o_ref.dtype)

    @pl.when(pl.program_id(1) != 0)
    def _():
        acc = jnp.dot(h_ref[...], w_ref[...].astype(BF16), preferred_element_type=F32)
        o_ref[...] = (acc * cs_ref[...]).astype(o_ref.dtype)


def _project(x2d, g, w, col_scale, n, tm, tn):
    m, d = x2d.shape
    n_i, n_j = m // tm, n // tn
    assert n_j >= X_PARTS

    def x_part(q):
        def index(i, j):
            nxt = jnp.minimum(i + (j >= n_j - X_PARTS + q).astype(jnp.int32), n_i - 1)
            return (nxt * X_PARTS + q, 0)
        return pl.BlockSpec((tm // X_PARTS, d), index)

    return pl.pallas_call(
        _proj_kernel,
        name="proj",
        grid=(n_i, n_j),
        in_specs=[
            *[x_part(q) for q in range(X_PARTS)],
            pl.BlockSpec((1, d), lambda i, j: (0, 0)),
            pl.BlockSpec((d, tn), lambda i, j: (0, j)),
            pl.BlockSpec((1, tn), lambda i, j: (0, j)),
        ],
        out_specs=[pl.BlockSpec((tm, tn), lambda i, j: (i, j)),
                   pl.BlockSpec((tm, d), lambda i, j: (i, 0))],
        out_shape=[jax.ShapeDtypeStruct((m, n), BF16), jax.ShapeDtypeStruct((m, d), BF16)],
        compiler_params=pltpu.CompilerParams(
            dimension_semantics=("arbitrary", "arbitrary"),
            vmem_limit_bytes=VMEM_LIMIT_V7X),
    )(*[x2d] * X_PARTS, g, w, col_scale)


def _t5_bucket_np(n):
    max_exact = N_BUCKETS // 2
    nf = np.maximum(n, 1).astype(np.float32)
    val = (np.log(nf / max_exact) / np.float32(math.log(MAX_DISTANCE / max_exact))
           * (N_BUCKETS - max_exact))
    large = np.minimum(max_exact + val.astype(np.int32), N_BUCKETS - 1)
    return np.where(n < max_exact, n, large).astype(np.int32)


def _bucket_row():
    return _t5_bucket_np(np.arange(2 * MOBA_BLOCK))[None, :].astype(np.int32)


def _bias_kernel(rb_ref, bkt_ref, o_ref):
    L = MOBA_BLOCK
    bkt = bkt_ref[...]
    key = lax.broadcasted_iota(jnp.int32, (L, L), 0)
    qry = lax.broadcasted_iota(jnp.int32, (L, L), 1)
    for h in range(N_HEADS):
        f = jnp.zeros(bkt.shape, F32)
        for b in range(N_BUCKETS):
            f = jnp.where(bkt == b, rb_ref[b, h] * LOG2E, f)
        both = pltpu.roll(jnp.broadcast_to(f, (L, 2 * L)), 0, 1, stride=1, stride_axis=0)
        o_ref[h, 0] = jnp.where(qry >= key, both[:, :L], NEG_INF)
        o_ref[h, 1] = both[:, L:]


def _bias_tables(rel_bias):
    bkt = jnp.asarray(_bucket_row())
    L = MOBA_BLOCK
    return pl.pallas_call(
        _bias_kernel,
        name="t5_bias",
        in_specs=[
            pl.BlockSpec(memory_space=pltpu.SMEM),
            pl.BlockSpec(memory_space=pltpu.VMEM),
        ],
        out_specs=pl.BlockSpec(memory_space=pltpu.VMEM),
        out_shape=jax.ShapeDtypeStruct((N_HEADS, 2, L, L), F32),
    )(rel_bias, bkt)


def _moba_sweeps(rb_ref, q_ref, k_ref, v_ref, z_ref, tab_ref, o_ref, vt_ref, s_ref):
    L = MOBA_BLOCK
    nb = k_ref.shape[0] // L
    h = pl.program_id(1)

    kf = k_ref[...].astype(F32).reshape(nb, L, HEAD_DIM)
    km = jnp.mean(kf, axis=1)
    km_hi = km.astype(BF16)
    km_lo = (km - km_hi.astype(F32)).astype(BF16)
    km2 = jnp.concatenate([km_hi, km_lo], axis=0)
    vt_ref[:HEAD_DIM, :] = v_ref[...].T
    ones_row = lax.broadcasted_iota(jnp.int32, (VT_PAD, vt_ref.shape[1]), 0) == 0
    vt_ref[HEAD_DIM:, :] = jnp.where(ones_row, 1.0, 0.0).astype(BF16)
    far_bias = rb_ref[N_BUCKETS - 1, h] * LOG2E
    row = lax.broadcasted_iota(jnp.int32, (nb, L), 0)

    def scores(qi, slot, out):
        q = q_ref[qi * L:(qi + 1) * L, :]
        sel_add = None
        if qi > MOBA_TOPK:
            g2 = lax.dot_general(km2, q, NT_DIMS, preferred_element_type=F32)
            g = g2[:nb] + g2[nb:]
            cnt = jnp.zeros((nb, L), jnp.int32)
            for mm in range(qi):
                gm = g[mm:mm + 1, :]
                ge = jnp.where(gm >= g, 1, 0)
                gt = jnp.where(gm > g, 1, 0)
                cnt = cnt + jnp.where(row > mm, ge, gt)
            sel_add = jnp.where(cnt < MOBA_TOPK, 0.0, NEG_INF)

        sbuf = s_ref.at[slot]
        shifts = []
        m = None
        for n in range(qi + 1):
            s = lax.dot_general(k_ref[n * L:(n + 1) * L, :], q, NT_DIMS,
                                preferred_element_type=F32)
            shift = None
            if n == qi:
                s = s + tab_ref[0]
            else:
                if n == qi - 1:
                    s = s + tab_ref[1]
                if sel_add is not None:
                    shift = sel_add[n:n + 1, :]
                if n < qi - 1:
                    shift = far_bias if shift is None else shift + far_bias
            sbuf[n * L:(n + 1) * L, :] = s
            tmax = jnp.max(s, axis=0, keepdims=True)
            if shift is not None:
                tmax = tmax + shift
            shifts.append(shift)
            m = tmax if m is None else jnp.maximum(m, tmax)
            yield
        out.append((m, shifts))

    def outputs(qi, slot, m, shifts):
        sbuf = s_ref.at[slot]
        acc = None
        for n in range(qi + 1):
            off = m if shifts[n] is None else m - shifts[n]
            p = jnp.exp2((sbuf[n * L:(n + 1) * L, :] - off).astype(BF16))
            pv = jnp.dot(vt_ref[:, n * L:(n + 1) * L], p,
                         preferred_element_type=F32)
            acc = pv if acc is None else acc + pv
            yield
        l = acc[HEAD_DIM:HEAD_DIM + 1, :]
        y = (acc[:HEAD_DIM, :] * (1.0 / l)).T
        z = z_ref[qi * L:(qi + 1) * L, :].astype(F32)
        o_ref[qi * L:(qi + 1) * L, :] = (y * _silu(z)).astype(o_ref.dtype)

    width = len(MOBA_GROUPS[0])
    pending = []
    for t, grp in enumerate(MOBA_GROUPS):
        outs = [[] for _ in grp]
        slots = [(t % 2) * width + i for i in range(width)]
        yield from _sweeps([scores(qi, slots[i], outs[i]) for i, qi in enumerate(grp)]
                           + [outputs(*args) for args in pending])
        pending = [(qi, slots[i]) + outs[i][0] for i, qi in enumerate(grp)]
    yield from _sweeps([outputs(*args) for args in pending])


def _retention_tables(S):
    C = RET_CHUNK
    dk = HEAD_DIM
    pos = jnp.arange(S, dtype=F32)
    theta = 1.0 / (ROT_BASE ** jnp.linspace(0.0, 1.0, dk // 2, dtype=F32))
    ang = pos[:, None] * theta[None, :]
    cos = jnp.repeat(jnp.cos(ang), 2, axis=1)
    sin = jnp.repeat(jnp.sin(ang), 2, axis=1) * jnp.tile(jnp.array([-1.0, 1.0], F32), dk // 2)
    log_gamma = jnp.log(1.0 - 2.0 ** (-5.0 - jnp.arange(N_HEADS, dtype=F32)))
    i = jnp.arange(C)
    diff = i[:, None] - i[None, :]
    k_scale = dk ** -0.5
    decay = jnp.where(diff >= 0,
                      jnp.exp(jnp.maximum(diff, 0).astype(F32) * log_gamma[:, None, None]),
                      0.0) * k_scale
    zeta = jnp.exp((C - 1 - i).astype(F32)[None, :] * log_gamma[:, None]) * k_scale
    xi = jnp.exp((i + 1).astype(F32)[None, :] * log_gamma[:, None])
    zeta = jnp.broadcast_to(zeta[:, :, None], (N_HEADS, C, dk))
    xi = jnp.broadcast_to(xi[:, :, None], (N_HEADS, C, dk))
    chunk_decay = jnp.exp(C * log_gamma)
    return cos, sin, decay, zeta, xi, chunk_decay


def _ret_chains(cd_ref, q_ref, k_ref, v_ref, z_ref, cos_ref, sin_ref, dm_ref, zeta_ref, xi_ref,
                o_ref, r_ref, first):
    C = RET_CHUNK
    n_chunks = q_ref.shape[0] // C

    @pl.when(first)
    def _():
        r_ref[...] = jnp.zeros_like(r_ref)

    lane = lax.broadcasted_iota(jnp.int32, (C, HEAD_DIM), 1)
    even = (lane % 2) == 0

    def rot(t, cosv, sinv):
        partner = jnp.where(even, pltpu.roll(t, HEAD_DIM - 1, 1), pltpu.roll(t, 1, 1))
        return t * cosv + partner * sinv

    def head(h):
        cols = slice(h * HEAD_DIM, (h + 1) * HEAD_DIM)
        for c in range(n_chunks):
            rows = slice(c * C, (c + 1) * C)
            cosv = cos_ref[rows, :]
            sinv = sin_ref[rows, :]
            q = rot(q_ref[rows, cols].astype(F32), cosv, sinv)
            k = rot(k_ref[rows, cols].astype(F32), cosv, sinv)
            v = v_ref[rows, cols]
            qb = q.astype(BF16)
            yield
            inner = lax.dot_general(qb, k.astype(BF16), NT_DIMS,
                                    preferred_element_type=F32) * dm_ref[h]
            r_old = r_ref[h]
            cross = xi_ref[h] * jnp.dot(qb, r_old.astype(BF16), preferred_element_type=F32)
            kz = (k * zeta_ref[h]).astype(BF16)
            r_ref[h] = r_old * cd_ref[h] + lax.dot_general(kz, v, TN_DIMS,
                                                           preferred_element_type=F32)
            yield
            o = jnp.dot(inner.astype(BF16), v, preferred_element_type=F32) + cross
            yield
            o = o * lax.rsqrt(jnp.mean(o * o, axis=-1, keepdims=True) + EPS)
            o_ref[rows, cols] = (o * _silu(z_ref[rows, cols].astype(F32))).astype(o_ref.dtype)
            yield

    return [head(h) for h in range(N_HEADS)]


def _attn_kernel(n_cast, rb_ref, qa_ref, ka_ref, va_ref, za_ref, tab_ref,
                 cd_ref, mixb_ref, rope_ref, dm_ref, zeta_ref, xi_ref, hx_ref, wg_ref, *refs):
    w32_refs = refs[:n_cast]
    ya_ref, yb_ref, gate_ref = refs[n_cast:n_cast + 3]
    w16_refs = refs[n_cast + 3:2 * n_cast + 3]
    vt_ref, s_ref, r_ref = refs[2 * n_cast + 3:]

    def gate_proj():
        tm, tn = gate_ref.shape
        for _ in range(GATE_DELAY):
            yield
        for r0 in range(0, tm, GATE_ROWS):
            for c0 in range(0, tn, GATE_COLS):
                acc = jnp.dot(hx_ref[r0:r0 + GATE_ROWS, :],
                              wg_ref[:, c0:c0 + GATE_COLS].astype(BF16),
                              preferred_element_type=F32)
                gate_ref[r0:r0 + GATE_ROWS, c0:c0 + GATE_COLS] = acc.astype(gate_ref.dtype)
                for _ in range(GATE_EVERY):
                    yield

    def casts():
        for w32, w16 in zip(w32_refs, w16_refs):
            w16[...] = w32[...].astype(BF16)
            yield

    qb_ref, kb_ref, vb_ref, zb_ref = (mixb_ref.at[:, c * D_A:(c + 1) * D_A] for c in range(4))
    cos_ref, sin_ref = (rope_ref.at[:, c * HEAD_DIM:(c + 1) * HEAD_DIM] for c in range(2))
    heads = _ret_chains(cd_ref, qb_ref, kb_ref, vb_ref, zb_ref, cos_ref, sin_ref, dm_ref, zeta_ref,
                        xi_ref, yb_ref, r_ref, first=pl.program_id(1) == 0)
    lanes = [_chain(heads[i::RET_LANES]) for i in range(RET_LANES)]
    moba = _moba_sweeps(rb_ref, qa_ref, ka_ref, va_ref, za_ref, tab_ref, ya_ref, vt_ref, s_ref)
    _round_robin([moba] + lanes + [casts(), gate_proj()])


def _attention(proj3, tabs, rel_bias, weights, hx, w_in, gate_col0, tm, tn):
    B, S, _ = proj3.shape
    L = MOBA_BLOCK
    C = RET_CHUNK
    dk = HEAD_DIM
    rows_b = S // N_HEADS
    assert sorted(qi for grp in MOBA_GROUPS for qi in grp) == list(range(S // L))
    assert rows_b % C == 0
    n_steps = B * N_HEADS
    gate_tiles = 2 * D_MODEL // tn
    assert (B * S // tm) * gate_tiles == n_steps
    cos, sin, decay, zeta, xi, chunk_decay = _retention_tables(S)

    def gate_tile(b, h):
        step = b * N_HEADS + h
        return step // gate_tiles, step % gate_tiles

    def head_cols(col0):
        return pl.BlockSpec((None, S, HEAD_DIM), lambda b, h: (b, 0, col0 + h))

    def all_heads(col0):
        return pl.BlockSpec((None, rows_b, D_A), lambda b, h: (b, h, col0))

    def w_slice(w):
        rows = max(w.shape[0] // n_steps, BF16_SUBLANES)
        n_slices = w.shape[0] // rows
        assert rows * n_slices == w.shape[0] and n_slices <= n_steps, w.shape
        return pl.BlockSpec((rows, w.shape[1]),
                            lambda b, h: (jnp.minimum(b * N_HEADS + h, n_slices - 1), 0))

    head_tab = pl.BlockSpec((N_HEADS, C, dk), lambda b, h: (0, 0, 0))
    w_specs = [w_slice(w) for w in weights]
    smem = pl.BlockSpec(memory_space=pltpu.SMEM)
    outs = pl.pallas_call(
        functools.partial(_attn_kernel, len(weights)),
        name="attention",
        grid=(B, N_HEADS),
        in_specs=[
            smem,
            head_cols(COL_QA), head_cols(COL_KA), head_cols(COL_VA), head_cols(COL_ZA),
            pl.BlockSpec((None, 2, L, L), lambda b, h: (h, 0, 0, 0)),
            smem,
            pl.BlockSpec((None, rows_b, 4 * D_A), lambda b, h: (b, h, COL_MIXB)),
            pl.BlockSpec((rows_b, 2 * dk), lambda b, h: (h, 0)),
            head_tab, head_tab, head_tab,
            pl.BlockSpec((tm, hx.shape[1]), lambda b, h: (gate_tile(b, h)[0], 0)),
            pl.BlockSpec((w_in.shape[0], tn),
                         lambda b, h: (0, gate_col0 // tn + gate_tile(b, h)[1])),
            *w_specs,
        ],
        out_specs=[head_cols(0), all_heads(0), pl.BlockSpec((tm, tn), gate_tile), *w_specs],
        out_shape=[jax.ShapeDtypeStruct((B, S, D_A), BF16)] * 2
        + [jax.ShapeDtypeStruct((B * S, 2 * D_MODEL), BF16)]
        + [jax.ShapeDtypeStruct(w.shape, BF16) for w in weights],
        scratch_shapes=[
            pltpu.VMEM((HEAD_DIM + VT_PAD, S), BF16),
            pltpu.VMEM((2 * len(MOBA_GROUPS[0]), S, L), F32),
            pltpu.VMEM((N_HEADS, dk, dk), F32),
        ],
        compiler_params=pltpu.CompilerParams(
            dimension_semantics=("arbitrary", "arbitrary"),
            vmem_limit_bytes=VMEM_LIMIT_V7X),
    )(rel_bias, proj3, proj3, proj3, proj3, tabs,
      chunk_decay, proj3, jnp.concatenate([cos, sin], axis=1), decay, zeta, xi, hx, w_in, *weights)
    return outs[0], outs[1], outs[2], outs[3:]


def _tail_kernel(ya_ref, yb_ref, ga_ref, gb_ref, x_ref, p_ref, wa_ref, wb_ref, wo_ref, wpg_ref,
                 wpp_ref, gple_ref, gfin_ref, o_ref):
    ua = jnp.dot(ya_ref[...], wa_ref[...], preferred_element_type=F32)
    ub = jnp.dot(yb_ref[...], wb_ref[...], preferred_element_type=F32)
    merged = (jax.nn.sigmoid(ga_ref[...].astype(F32)) * ua
              + jax.nn.sigmoid(gb_ref[...].astype(F32)) * ub)
    x1 = x_ref[...] + jnp.dot(merged.astype(BF16), wo_ref[...], preferred_element_type=F32)
    hp = _rms_norm(x1, gple_ref[...]).astype(BF16)
    gate = jax.nn.sigmoid(jnp.dot(hp, wpg_ref[...], preferred_element_type=F32))
    pe = jnp.dot(p_ref[...].astype(BF16), wpp_ref[...], preferred_element_type=F32)
    x2 = x1 + gate * pe
    o_ref[...] = _rms_norm(x2, gfin_ref[...])


def _tail(ya, yb, gates, x2d, p2d, wa, wb, wo, wpg, wpp, g_ple, g_final, tm):
    m, d = x2d.shape

    def const(shape):
        return pl.BlockSpec(shape, lambda i: (0, 0), pipeline_mode=pl.Buffered(1))

    return pl.pallas_call(
        _tail_kernel,
        name="tail",
        grid=(m // tm,),
        in_specs=[
            pl.BlockSpec((tm, D_A), lambda i: (i, 0)),
            pl.BlockSpec((tm, D_A), lambda i: (i, 0)),
            pl.BlockSpec((tm, d), lambda i: (i, 0)),
            pl.BlockSpec((tm, d), lambda i: (i, 1)),
            pl.BlockSpec((tm, d), lambda i: (i, 0)),
            pl.BlockSpec((tm, D_PLE), lambda i: (i, 0)),
            const((D_A, d)), const((D_A, d)), const((d, d)), const((d, d)), const((D_PLE, d)),
            const((1, d)), const((1, d)),
        ],
        out_specs=pl.BlockSpec((tm, d), lambda i: (i, 0)),
        out_shape=jax.ShapeDtypeStruct((m, d), F32),
        compiler_params=pltpu.CompilerParams(
            dimension_semantics=("parallel",),
            vmem_limit_bytes=VMEM_LIMIT_V7X),
    )(ya, yb, gates, gates, x2d, p2d, wa, wb, wo, wpg, wpp, g_ple, g_final)


def kernel(x, p, g_mix, w_in, w_a, w_b, w_out, g_ple, w_ple_gate, w_ple_proj, rel_bias, g_final):
    B, S, d = x.shape
    assert w_in.shape[0] == 1, "the fused tail applies the final norm: single-layer stacks only"
    tabs = _bias_tables(rel_bias)
    x2d = x.reshape(B * S, d)
    n_mix = 8 * D_A
    col_scale = jnp.where(jnp.arange(n_mix) < D_A, HEAD_DIM ** -0.5 * LOG2E, 1.0).astype(F32)
    proj, hx = _project(x2d, g_mix, w_in[0], col_scale[None, :], n_mix, tm=1024, tn=1024)
    proj3 = proj.reshape(B, S, n_mix)
    ya, yb, gates, (wa, wb, wo, wpg, wpp) = _attention(
        proj3, tabs, rel_bias, (w_a[0], w_b[0], w_out[0], w_ple_gate[0], w_ple_proj[0]),
        hx, w_in[0], n_mix, tm=1024, tn=1024)
    out = _tail(ya.reshape(B * S, D_A), yb.reshape(B * S, D_A), gates, x2d,
                p[0].reshape(B * S, D_PLE), wa, wb, wo, wpg, wpp,
                g_ple, g_final[None, :], tm=256)
    return out.reshape(B, S, d)
```

```python
import functools
import math

import numpy as np
import jax
import jax.numpy as jnp
from jax import lax
from jax.experimental import pallas as pl
from jax.experimental.pallas import tpu as pltpu

D_MODEL = 2048
N_HEADS = 8
HEAD_DIM = 128
D_A = N_HEADS * HEAD_DIM
MOBA_BLOCK = 256
MOBA_TOPK = 3
N_BUCKETS = 32
MAX_DISTANCE = 128
RET_CHUNK = 128
ROT_BASE = 10000.0
D_PLE = 256
EPS = 1e-6

COL_QA, COL_KA, COL_VA, COL_ZA = (i * N_HEADS for i in range(4))
COL_MIXB = 1

VMEM_LIMIT_V7X = 56 * 1024 * 1024
PROJ_TM = 1024
PROJ_TN = 1024
TAIL_TM = 256
BF16_SUBLANES = 16
VT_PAD = BF16_SUBLANES
MOBA_GROUPS = ((7, 0), (6, 1), (5, 2), (4, 3))
RET_LANES = 2
X_PARTS = 4
GATE_ROWS = 1024
GATE_COLS = 256
GATE_EVERY = 8
GATE_DELAY = 2

F32 = jnp.float32
BF16 = jnp.bfloat16
NEG_INF = float("-inf")
LOG2E = math.log2(math.e)
NT_DIMS = (((1,), (1,)), ((), ()))
TN_DIMS = (((0,), (0,)), ((), ()))


def _rms_norm(x, g):
    return x * lax.rsqrt(jnp.mean(x * x, axis=-1, keepdims=True) + EPS) * g


def _silu(z):
    return z * jax.nn.sigmoid(z)


def _sweeps(gens):
    gens = list(gens)
    while gens:
        for g in list(gens):
            try:
                next(g)
            except StopIteration:
                gens.remove(g)
        yield


def _round_robin(gens):
    for _ in _sweeps(gens):
        pass


def _chain(gens):
    for g in gens:
        yield from g


def _proj_kernel(*refs):
    x_refs, (g_ref, w_ref, cs_ref, o_ref, h_ref) = refs[:X_PARTS], refs[X_PARTS:]

    rows = h_ref.shape[0] // X_PARTS

    @pl.when(pl.program_id(1) == 0)
    def _():
        w = w_ref[...].astype(BF16)
        for q, x_ref in enumerate(x_refs):
            part = slice(q * rows, (q + 1) * rows)
            hq = _rms_norm(x_ref[...], g_ref[...]).astype(BF16)
            h_ref[part, :] = hq
            acc = jnp.dot(hq, w, preferred_element_type=F32)
            o_ref[part, :] = (acc * cs_ref[...]).astype(o_ref.dtype)

    @pl.when(pl.program_id(1) != 0)
    def _():
        acc = jnp.dot(h_ref[...], w_ref[...].astype(BF16), preferred_element_type=F32)
        o_ref[...] = (acc * cs_ref[...]).astype(o_ref.dtype)


def _project(x2d, g, w, col_scale, n, tm, tn):
    m, d = x2d.shape
    n_i, n_j = m // tm, n // tn
    assert n_j >= X_PARTS

    def x_part(q):
        def index(i, j):
            nxt = jnp.minimum(i + (j >= n_j - X_PARTS + q).astype(jnp.int32), n_i - 1)
            return (nxt * X_PARTS + q, 0)
        return pl.BlockSpec((tm // X_PARTS, d), index)

    return pl.pallas_call(
        _proj_kernel,
        name="proj",
        grid=(n_i, n_j),
        in_specs=[
            *[x_part(q) for q in range(X_PARTS)],
            pl.BlockSpec((1, d), lambda i, j: (0, 0)),
            pl.BlockSpec((d, tn), lambda i, j: (0, j)),
            pl.BlockSpec((1, tn), lambda i, j: (0, j)),
        ],
        out_specs=[pl.BlockSpec((tm, tn), lambda i, j: (i, j)),
                   pl.BlockSpec((tm, d), lambda i, j: (i, 0))],
        out_shape=[jax.ShapeDtypeStruct((m, n), BF16), jax.ShapeDtypeStruct((m, d), BF16)],
        compiler_params=pltpu.CompilerParams(
            dimension_semantics=("arbitrary", "arbitrary"),
            vmem_limit_bytes=VMEM_LIMIT_V7X),
    )(*[x2d] * X_PARTS, g, w, col_scale)


def _t5_bucket_np(n):
    max_exact = N_BUCKETS // 2
    nf = np.maximum(n, 1).astype(np.float32)
    val = (np.log(nf / max_exact) / np.float32(math.log(MAX_DISTANCE / max_exact))
           * (N_BUCKETS - max_exact))
    large = np.minimum(max_exact + val.astype(np.int32), N_BUCKETS - 1)
    return np.where(n < max_exact, n, large).astype(np.int32)


def _bucket_row():
    return _t5_bucket_np(np.arange(2 * MOBA_BLOCK))[None, :].astype(np.int32)


def _bias_kernel(rb_ref, bkt_ref, o_ref):
    L = MOBA_BLOCK
    bkt = bkt_ref[...]
    key = lax.broadcasted_iota(jnp.int32, (L, L), 0)
    qry = lax.broadcasted_iota(jnp.int32, (L, L), 1)
    for h in range(N_HEADS):
        f = jnp.zeros(bkt.shape, F32)
        for b in range(N_BUCKETS):
            f = jnp.where(bkt == b, rb_ref[b, h] * LOG2E, f)
        both = pltpu.roll(jnp.broadcast_to(f, (L, 2 * L)), 0, 1, stride=1, stride_axis=0)
        o_ref[h, 0] = jnp.where(qry >= key, both[:, :L], NEG_INF)
        o_ref[h, 1] = both[:, L:]


def _bias_tables(rel_bias):
    bkt = jnp.asarray(_bucket_row())
    L = MOBA_BLOCK
    return pl.pallas_call(
        _bias_kernel,
        name="t5_bias",
        in_specs=[
            pl.BlockSpec(memory_space=pltpu.SMEM),
            pl.BlockSpec(memory_space=pltpu.VMEM),
        ],
        out_specs=pl.BlockSpec(memory_space=pltpu.VMEM),
        out_shape=jax.ShapeDtypeStruct((N_HEADS, 2, L, L), F32),
    )(rel_bias, bkt)


def _moba_sweeps(rb_ref, q_ref, k_ref, v_ref, z_ref, tab_ref, o_ref, vt_ref, s_ref):
    L = MOBA_BLOCK
    nb = k_ref.shape[0] // L
    h = pl.program_id(1)

    kf = k_ref[...].astype(F32).reshape(nb, L, HEAD_DIM)
    km = jnp.mean(kf, axis=1)
    km_hi = km.astype(BF16)
    km_lo = (km - km_hi.astype(F32)).astype(BF16)
    km2 = jnp.concatenate([km_hi, km_lo], axis=0)
    vt_ref[:HEAD_DIM, :] = v_ref[...].T
    ones_row = lax.broadcasted_iota(jnp.int32, (VT_PAD, vt_ref.shape[1]), 0) == 0
    vt_ref[HEAD_DIM:, :] = jnp.where(ones_row, 1.0, 0.0).astype(BF16)
    far_bias = rb_ref[N_BUCKETS - 1, h] * LOG2E
    row = lax.broadcasted_iota(jnp.int32, (nb, L), 0)

    def scores(qi, slot, out):
        q = q_ref[qi * L:(qi + 1) * L, :]
        sel_add = None
        if qi > MOBA_TOPK:
            g2 = lax.dot_general(km2, q, NT_DIMS, preferred_element_type=F32)
            g = g2[:nb] + g2[nb:]
            cnt = jnp.zeros((nb, L), jnp.int32)
            for mm in range(qi):
                gm = g[mm:mm + 1, :]
                ge = jnp.where(gm >= g, 1, 0)
                gt = jnp.where(gm > g, 1, 0)
                cnt = cnt + jnp.where(row > mm, ge, gt)
            sel_add = jnp.where(cnt < MOBA_TOPK, 0.0, NEG_INF)

        sbuf = s_ref.at[slot]
        shifts = []
        m = None
        for n in range(qi + 1):
            s = lax.dot_general(k_ref[n * L:(n + 1) * L, :], q, NT_DIMS,
                                preferred_element_type=F32)
            shift = None
            if n == qi:
                s = s + tab_ref[0]
            else:
                if n == qi - 1:
                    s = s + tab_ref[1]
                if sel_add is not None:
                    shift = sel_add[n:n + 1, :]
                if n < qi - 1:
                    shift = far_bias if shift is None else shift + far_bias
            sbuf[n * L:(n + 1) * L, :] = s
            tmax = jnp.max(s, axis=0, keepdims=True)
            if shift is not None:
                tmax = tmax + shift
            shifts.append(shift)
            m = tmax if m is None else jnp.maximum(m, tmax)
            yield
        out.append((m, shifts))

    def outputs(qi, slot, m, shifts):
        sbuf = s_ref.at[slot]
        acc = None
        for n in range(qi + 1):
            off = m if shifts[n] is None else m - shifts[n]
            p = jnp.exp2((sbuf[n * L:(n + 1) * L, :] - off).astype(BF16))
            pv = jnp.dot(vt_ref[:, n * L:(n + 1) * L], p,
                         preferred_element_type=F32)
            acc = pv if acc is None else acc + pv
            yield
        l = acc[HEAD_DIM:HEAD_DIM + 1, :]
        y = (acc[:HEAD_DIM, :] * (1.0 / l)).T
        z = z_ref[qi * L:(qi + 1) * L, :].astype(F32)
        o_ref[qi * L:(qi + 1) * L, :] = (y * _silu(z)).astype(o_ref.dtype)

    width = len(MOBA_GROUPS[0])
    pending = []
    for t, grp in enumerate(MOBA_GROUPS):
        outs = [[] for _ in grp]
        slots = [(t % 2) * width + i for i in range(width)]
        yield from _sweeps([scores(qi, slots[i], outs[i]) for i, qi in enumerate(grp)]
                           + [outputs(*args) for args in pending])
        pending = [(qi, slots[i]) + outs[i][0] for i, qi in enumerate(grp)]
    yield from _sweeps([outputs(*args) for args in pending])


def _retention_tables(S):
    C = RET_CHUNK
    dk = HEAD_DIM
    pos = jnp.arange(S, dtype=F32)
    theta = 1.0 / (ROT_BASE ** jnp.linspace(0.0, 1.0, dk // 2, dtype=F32))
    ang = pos[:, None] * theta[None, :]
    cos = jnp.repeat(jnp.cos(ang), 2, axis=1)
    sin = jnp.repeat(jnp.sin(ang), 2, axis=1) * jnp.tile(jnp.array([-1.0, 1.0], F32), dk // 2)
    log_gamma = jnp.log(1.0 - 2.0 ** (-5.0 - jnp.arange(N_HEADS, dtype=F32)))
    i = jnp.arange(C)
    diff = i[:, None] - i[None, :]
    k_scale = dk ** -0.5
    decay = jnp.where(diff >= 0,
                      jnp.exp(jnp.maximum(diff, 0).astype(F32) * log_gamma[:, None, None]),
                      0.0) * k_scale
    zeta = jnp.exp((C - 1 - i).astype(F32)[None, :] * log_gamma[:, None]) * k_scale
    xi = jnp.exp((i + 1).astype(F32)[None, :] * log_gamma[:, None])
    zeta = jnp.broadcast_to(zeta[:, :, None], (N_HEADS, C, dk))
    xi = jnp.broadcast_to(xi[:, :, None], (N_HEADS, C, dk))
    chunk_decay = jnp.exp(C * log_gamma)
    return cos, sin, decay, zeta, xi, chunk_decay


def _ret_chains(cd_ref, q_ref, k_ref, v_ref, z_ref, cos_ref, sin_ref, dm_ref, zeta_ref, xi_ref,
                o_ref, r_ref, first):
    C = RET_CHUNK
    n_chunks = q_ref.shape[0] // C

    @pl.when(first)
    def _():
        r_ref[...] = jnp.zeros_like(r_ref)

    lane = lax.broadcasted_iota(jnp.int32, (C, HEAD_DIM), 1)
    even = (lane % 2) == 0

    def rot(t, cosv, sinv):
        partner = jnp.where(even, pltpu.roll(t, HEAD_DIM - 1, 1), pltpu.roll(t, 1, 1))
        return t * cosv + partner * sinv

    def head(h):
        cols = slice(h * HEAD_DIM, (h + 1) * HEAD_DIM)
        for c in range(n_chunks):
            rows = slice(c * C, (c + 1) * C)
            cosv = cos_ref[rows, :]
            sinv = sin_ref[rows, :]
            q = rot(q_ref[rows, cols].astype(F32), cosv, sinv)
            k = rot(k_ref[rows, cols].astype(F32), cosv, sinv)
            v = v_ref[rows, cols]
            qb = q.astype(BF16)
            yield
            inner = lax.dot_general(qb, k.astype(BF16), NT_DIMS,
                                    preferred_element_type=F32) * dm_ref[h]
            r_old = r_ref[h]
            kz = (k * zeta_ref[h]).astype(BF16)
            r_ref[h] = r_old * cd_ref[h] + lax.dot_general(kz, v, TN_DIMS,
                                                           preferred_element_type=F32)
            yield
            lhs = jnp.concatenate([inner.astype(BF16), (q * xi_ref[h]).astype(BF16)], axis=1)
            rhs = jnp.concatenate([v, r_old.astype(BF16)], axis=0)
            o = jnp.dot(lhs, rhs, preferred_element_type=F32)
            yield
            o = o * lax.rsqrt(jnp.mean(o * o, axis=-1, keepdims=True) + EPS)
            o_ref[rows, cols] = (o * _silu(z_ref[rows, cols].astype(F32))).astype(o_ref.dtype)
            yield

    return [head(h) for h in range(N_HEADS)]


def _attn_kernel(n_cast, rb_ref, qa_ref, ka_ref, va_ref, za_ref, tab_ref,
                 cd_ref, mixb_ref, rope_ref, dm_ref, zeta_ref, xi_ref, hx_ref, wg_ref, *refs):
    w32_refs = refs[:n_cast]
    ya_ref, yb_ref, gate_ref = refs[n_cast:n_cast + 3]
    w16_refs = refs[n_cast + 3:2 * n_cast + 3]
    vt_ref, s_ref, r_ref = refs[2 * n_cast + 3:]

    def gate_proj():
        tm, tn = gate_ref.shape
        for _ in range(GATE_DELAY):
            yield
        for r0 in range(0, tm, GATE_ROWS):
            for c0 in range(0, tn, GATE_COLS):
                acc = jnp.dot(hx_ref[r0:r0 + GATE_ROWS, :],
                              wg_ref[:, c0:c0 + GATE_COLS].astype(BF16),
                              preferred_element_type=F32)
                gate_ref[r0:r0 + GATE_ROWS, c0:c0 + GATE_COLS] = acc.astype(gate_ref.dtype)
                for _ in range(GATE_EVERY):
                    yield

    def casts():
        for w32, w16 in zip(w32_refs, w16_refs):
            w16[...] = w32[...].astype(BF16)
            yield

    qb_ref, kb_ref, vb_ref, zb_ref = (mixb_ref.at[:, c * D_A:(c + 1) * D_A] for c in range(4))
    cos_ref, sin_ref = (rope_ref.at[:, c * HEAD_DIM:(c + 1) * HEAD_DIM] for c in range(2))
    heads = _ret_chains(cd_ref, qb_ref, kb_ref, vb_ref, zb_ref, cos_ref, sin_ref, dm_ref, zeta_ref,
                        xi_ref, yb_ref, r_ref, first=pl.program_id(1) == 0)
    lanes = [_chain(heads[i::RET_LANES]) for i in range(RET_LANES)]
    moba = _moba_sweeps(rb_ref, qa_ref, ka_ref, va_ref, za_ref, tab_ref, ya_ref, vt_ref, s_ref)
    _round_robin([moba] + lanes + [casts(), gate_proj()])


def _attention(proj3, tabs, rel_bias, weights, hx, w_in, gate_col0, tm, tn):
    B, S, _ = proj3.shape
    L = MOBA_BLOCK
    C = RET_CHUNK
    dk = HEAD_DIM
    rows_b = S // N_HEADS
    assert sorted(qi for grp in MOBA_GROUPS for qi in grp) == list(range(S // L))
    assert rows_b % C == 0
    n_steps = B * N_HEADS
    gate_tiles = 2 * D_MODEL // tn
    assert (B * S // tm) * gate_tiles == n_steps
    cos, sin, decay, zeta, xi, chunk_decay = _retention_tables(S)

    def gate_tile(b, h):
        step = b * N_HEADS + h
        return step // gate_tiles, step % gate_tiles

    def head_cols(col0):
        return pl.BlockSpec((None, S, HEAD_DIM), lambda b, h: (b, 0, col0 + h))

    def all_heads(col0):
        return pl.BlockSpec((None, rows_b, D_A), lambda b, h: (b, h, col0))

    def w_slice(w):
        rows = max(w.shape[0] // n_steps, BF16_SUBLANES)
        n_slices = w.shape[0] // rows
        assert rows * n_slices == w.shape[0] and n_slices <= n_steps, w.shape
        return pl.BlockSpec((rows, w.shape[1]),
                            lambda b, h: (jnp.minimum(b * N_HEADS + h, n_slices - 1), 0))

    head_tab = pl.BlockSpec((N_HEADS, C, dk), lambda b, h: (0, 0, 0))
    w_specs = [w_slice(w) for w in weights]
    smem = pl.BlockSpec(memory_space=pltpu.SMEM)
    outs = pl.pallas_call(
        functools.partial(_attn_kernel, len(weights)),
        name="attention",
        grid=(B, N_HEADS),
        in_specs=[
            smem,
            head_cols(COL_QA), head_cols(COL_KA), head_cols(COL_VA), head_cols(COL_ZA),
            pl.BlockSpec((None, 2, L, L), lambda b, h: (h, 0, 0, 0)),
            smem,
            pl.BlockSpec((None, rows_b, 4 * D_A), lambda b, h: (b, h, COL_MIXB)),
            pl.BlockSpec((rows_b, 2 * dk), lambda b, h: (h, 0)),
            head_tab, head_tab, head_tab,
            pl.BlockSpec((tm, hx.shape[1]), lambda b, h: (gate_tile(b, h)[0], 0)),
            pl.BlockSpec((w_in.shape[0], tn),
                         lambda b, h: (0, gate_col0 // tn + gate_tile(b, h)[1])),
            *w_specs,
        ],
        out_specs=[head_cols(0), all_heads(0), pl.BlockSpec((tm, tn), gate_tile), *w_specs],
        out_shape=[jax.ShapeDtypeStruct((B, S, D_A), BF16)] * 2
        + [jax.ShapeDtypeStruct((B * S, 2 * D_MODEL), BF16)]
        + [jax.ShapeDtypeStruct(w.shape, BF16) for w in weights],
        scratch_shapes=[
            pltpu.VMEM((HEAD_DIM + VT_PAD, S), BF16),
            pltpu.VMEM((2 * len(MOBA_GROUPS[0]), S, L), F32),
            pltpu.VMEM((N_HEADS, dk, dk), F32),
        ],
        compiler_params=pltpu.CompilerParams(
            dimension_semantics=("arbitrary", "arbitrary"),
            vmem_limit_bytes=VMEM_LIMIT_V7X),
    )(rel_bias, proj3, proj3, proj3, proj3, tabs,
      chunk_decay, proj3, jnp.concatenate([cos, sin], axis=1), decay, zeta, xi, hx, w_in, *weights)
    return outs[0], outs[1], outs[2], outs[3:]


def _tail_kernel(ya_ref, yb_ref, ga_ref, gb_ref, x_ref, p_ref, wa_ref, wb_ref, wo_ref, wpg_ref,
                 wpp_ref, gple_ref, gfin_ref, o_ref):
    ua = jnp.dot(ya_ref[...], wa_ref[...], preferred_element_type=F32)
    ub = jnp.dot(yb_ref[...], wb_ref[...], preferred_element_type=F32)
    merged = (jax.nn.sigmoid(ga_ref[...].astype(F32)) * ua
              + jax.nn.sigmoid(gb_ref[...].astype(F32)) * ub)
    x1 = x_ref[...] + jnp.dot(merged.astype(BF16), wo_ref[...], preferred_element_type=F32)
    hp = _rms_norm(x1, gple_ref[...]).astype(BF16)
    gate = jax.nn.sigmoid(jnp.dot(hp, wpg_ref[...], preferred_element_type=F32))
    pe = jnp.dot(p_ref[...].astype(BF16), wpp_ref[...], preferred_element_type=F32)
    x2 = x1 + gate * pe
    o_ref[...] = _rms_norm(x2, gfin_ref[...])


def _tail(ya, yb, gates, x2d, p2d, wa, wb, wo, wpg, wpp, g_ple, g_final, tm):
    m, d = x2d.shape

    def const(shape):
        return pl.BlockSpec(shape, lambda i: (0, 0), pipeline_mode=pl.Buffered(1))

    return pl.pallas_call(
        _tail_kernel,
        name="tail",
        grid=(m // tm,),
        in_specs=[
            pl.BlockSpec((tm, D_A), lambda i: (i, 0)),
            pl.BlockSpec((tm, D_A), lambda i: (i, 0)),
            pl.BlockSpec((tm, d), lambda i: (i, 0)),
            pl.BlockSpec((tm, d), lambda i: (i, 1)),
            pl.BlockSpec((tm, d), lambda i: (i, 0)),
            pl.BlockSpec((tm, D_PLE), lambda i: (i, 0)),
            const((D_A, d)), const((D_A, d)), const((d, d)), const((d, d)), const((D_PLE, d)),
            const((1, d)), const((1, d)),
        ],
        out_specs=pl.BlockSpec((tm, d), lambda i: (i, 0)),
        out_shape=jax.ShapeDtypeStruct((m, d), F32),
        compiler_params=pltpu.CompilerParams(
            dimension_semantics=("parallel",),
            vmem_limit_bytes=VMEM_LIMIT_V7X),
    )(ya, yb, gates, gates, x2d, p2d, wa, wb, wo, wpg, wpp, g_ple, g_final)


def kernel(x, p, g_mix, w_in, w_a, w_b, w_out, g_ple, w_ple_gate, w_ple_proj, rel_bias, g_final):
    B, S, d = x.shape
    assert w_in.shape[0] == 1, "the fused tail applies the final norm: single-layer stacks only"
    tabs = _bias_tables(rel_bias)
    x2d = x.reshape(B * S, d)
    n_mix = 8 * D_A
    col_scale = jnp.where(jnp.arange(n_mix) < D_A, HEAD_DIM ** -0.5 * LOG2E, 1.0).astype(F32)
    proj, hx = _project(x2d, g_mix, w_in[0], col_scale[None, :], n_mix, PROJ_TM, PROJ_TN)
    proj3 = proj.reshape(B, S, n_mix)
    ya, yb, gates, (wa, wb, wo, wpg, wpp) = _attention(
        proj3, tabs, rel_bias, (w_a[0], w_b[0], w_out[0], w_ple_gate[0], w_ple_proj[0]),
        hx, w_in[0], n_mix, PROJ_TM, PROJ_TN)
    out = _tail(ya.reshape(B * S, D_A), yb.reshape(B * S, D_A), gates, x2d,
                p[0].reshape(B * S, D_PLE), wa, wb, wo, wpg, wpp,
                g_ple, g_final[None, :], TAIL_TM)
    return out.reshape(B, S, d)
```

```python
import functools
import math

import numpy as np
import jax
import jax.numpy as jnp
from jax import lax
from jax.experimental import pallas as pl
from jax.experimental.pallas import tpu as pltpu

D_MODEL = 2048
N_HEADS = 8
HEAD_DIM = 128
D_A = N_HEADS * HEAD_DIM
MOBA_BLOCK = 256
MOBA_TOPK = 3
N_BUCKETS = 32
MAX_DISTANCE = 128
RET_CHUNK = 128
ROT_BASE = 10000.0
D_PLE = 256
EPS = 1e-6

COL_QA, COL_KA, COL_VA, COL_ZA = (i * N_HEADS for i in range(4))
COL_MIXB = 1

VMEM_LIMIT_V7X = 56 * 1024 * 1024
PROJ_TM = 1024
PROJ_TN = 1024
TAIL_TM = 256
BF16_SUBLANES = 16
VT_PAD = BF16_SUBLANES
MOBA_GROUPS = ((7, 0), (6, 1), (5, 2), (4, 3))
RET_LANES = 2
X_PARTS = 4
GATE_ROWS = 1024
GATE_COLS = 256
GATE_EVERY = 8
GATE_DELAY = 2

F32 = jnp.float32
BF16 = jnp.bfloat16
NEG_INF = float("-inf")
LOG2E = math.log2(math.e)
NT_DIMS = (((1,), (1,)), ((), ()))
TN_DIMS = (((0,), (0,)), ((), ()))


def _rms_norm(x, g):
    return x * lax.rsqrt(jnp.mean(x * x, axis=-1, keepdims=True) + EPS) * g


def _silu(z):
    return z * jax.nn.sigmoid(z)


def _sweeps(gens):
    gens = list(gens)
    while gens:
        for g in list(gens):
            try:
                next(g)
            except StopIteration:
                gens.remove(g)
        yield


def _round_robin(gens):
    for _ in _sweeps(gens):
        pass


def _chain(gens):
    for g in gens:
        yield from g


def _proj_kernel(*refs):
    x_refs, (g_ref, w_ref, cs_ref, o_ref, h_ref) = refs[:X_PARTS], refs[X_PARTS:]

    rows = h_ref.shape[0] // X_PARTS

    @pl.when(pl.program_id(1) == 0)
    def _():
        w = w_ref[...].astype(BF16)
        for q, x_ref in enumerate(x_refs):
            part = slice(q * rows, (q + 1) * rows)
            hq = _rms_norm(x_ref[...], g_ref[...]).astype(BF16)
            h_ref[part, :] = hq
            acc = jnp.dot(hq, w, preferred_element_type=F32)
            o_ref[part, :] = (acc * cs_ref[...]).astype(o_ref.dtype)

    @pl.when(pl.program_id(1) != 0)
    def _():
        acc = jnp.dot(h_ref[...], w_ref[...].astype(BF16), preferred_element_type=F32)
        o_ref[...] = (acc * cs_ref[...]).astype(o_ref.dtype)


def _project(x2d, g, w, col_scale, n, tm, tn):
    m, d = x2d.shape
    n_i, n_j = m // tm, n // tn
    assert n_j >= X_PARTS

    def x_part(q):
        def index(i, j):
            nxt = jnp.minimum(i + (j >= n_j - X_PARTS + q).astype(jnp.int32), n_i - 1)
            return (nxt * X_PARTS + q, 0)
        return pl.BlockSpec((tm // X_PARTS, d), index)

    return pl.pallas_call(
        _proj_kernel,
        name="proj",
        grid=(n_i, n_j),
        in_specs=[
            *[x_part(q) for q in range(X_PARTS)],
            pl.BlockSpec((1, d), lambda i, j: (0, 0)),
            pl.BlockSpec((d, tn), lambda i, j: (0, j)),
            pl.BlockSpec((1, tn), lambda i, j: (0, j)),
        ],
        out_specs=[pl.BlockSpec((tm, tn), lambda i, j: (i, j)),
                   pl.BlockSpec((tm, d), lambda i, j: (i, 0))],
        out_shape=[jax.ShapeDtypeStruct((m, n), BF16), jax.ShapeDtypeStruct((m, d), BF16)],
        compiler_params=pltpu.CompilerParams(
            dimension_semantics=("arbitrary", "arbitrary"),
            vmem_limit_bytes=VMEM_LIMIT_V7X),
    )(*[x2d] * X_PARTS, g, w, col_scale)


def _t5_bucket_np(n):
    max_exact = N_BUCKETS // 2
    nf = np.maximum(n, 1).astype(np.float32)
    val = (np.log(nf / max_exact) / np.float32(math.log(MAX_DISTANCE / max_exact))
           * (N_BUCKETS - max_exact))
    large = np.minimum(max_exact + val.astype(np.int32), N_BUCKETS - 1)
    return np.where(n < max_exact, n, large).astype(np.int32)


def _bucket_row():
    return _t5_bucket_np(np.arange(2 * MOBA_BLOCK))[None, :].astype(np.int32)


def _bias_kernel(rb_ref, bkt_ref, o_ref):
    L = MOBA_BLOCK
    bkt = bkt_ref[...]
    key = lax.broadcasted_iota(jnp.int32, (L, L), 0)
    qry = lax.broadcasted_iota(jnp.int32, (L, L), 1)
    for h in range(N_HEADS):
        f = jnp.zeros(bkt.shape, F32)
        for b in range(N_BUCKETS):
            f = jnp.where(bkt == b, rb_ref[b, h] * LOG2E, f)
        both = pltpu.roll(jnp.broadcast_to(f, (L, 2 * L)), 0, 1, stride=1, stride_axis=0)
        o_ref[h, 0] = jnp.where(qry >= key, both[:, :L], NEG_INF)
        o_ref[h, 1] = both[:, L:]


def _bias_tables(rel_bias):
    bkt = jnp.asarray(_bucket_row())
    L = MOBA_BLOCK
    return pl.pallas_call(
        _bias_kernel,
        name="t5_bias",
        in_specs=[
            pl.BlockSpec(memory_space=pltpu.SMEM),
            pl.BlockSpec(memory_space=pltpu.VMEM),
        ],
        out_specs=pl.BlockSpec(memory_space=pltpu.VMEM),
        out_shape=jax.ShapeDtypeStruct((N_HEADS, 2, L, L), F32),
    )(rel_bias, bkt)


def _moba_sweeps(rb_ref, q_ref, k_ref, v_ref, z_ref, tab_ref, o_ref, vt_ref, s_ref):
    L = MOBA_BLOCK
    nb = k_ref.shape[0] // L
    h = pl.program_id(1)

    kf = k_ref[...].astype(F32).reshape(nb, L, HEAD_DIM)
    km = jnp.mean(kf, axis=1)
    km_hi = km.astype(BF16)
    km_lo = (km - km_hi.astype(F32)).astype(BF16)
    km2 = jnp.concatenate([km_hi, km_lo], axis=0)
    vt_ref[:HEAD_DIM, :] = v_ref[...].T
    ones_row = lax.broadcasted_iota(jnp.int32, (VT_PAD, vt_ref.shape[1]), 0) == 0
    vt_ref[HEAD_DIM:, :] = jnp.where(ones_row, 1.0, 0.0).astype(BF16)
    far_bias = rb_ref[N_BUCKETS - 1, h] * LOG2E
    row = lax.broadcasted_iota(jnp.int32, (nb, L), 0)

    def scores(qi, slot, out):
        q = q_ref[qi * L:(qi + 1) * L, :]
        sel_add = None
        if qi > MOBA_TOPK:
            g2 = lax.dot_general(km2, q, NT_DIMS, preferred_element_type=F32)
            g = g2[:nb] + g2[nb:]
            cnt = jnp.zeros((nb, L), jnp.int32)
            for mm in range(qi):
                gm = g[mm:mm + 1, :]
                ge = jnp.where(gm >= g, 1, 0)
                gt = jnp.where(gm > g, 1, 0)
                cnt = cnt + jnp.where(row > mm, ge, gt)
            sel_add = jnp.where(cnt < MOBA_TOPK, 0.0, NEG_INF)

        sbuf = s_ref.at[slot]
        shifts = []
        m = None
        for n in range(qi + 1):
            s = lax.dot_general(k_ref[n * L:(n + 1) * L, :], q, NT_DIMS,
                                preferred_element_type=F32)
            shift = None
            if n == qi:
                s = s + tab_ref[0]
            else:
                if n == qi - 1:
                    s = s + tab_ref[1]
                if sel_add is not None:
                    shift = sel_add[n:n + 1, :]
                if n < qi - 1:
                    shift = far_bias if shift is None else shift + far_bias
            sbuf[n * L:(n + 1) * L, :] = s
            tmax = jnp.max(s, axis=0, keepdims=True)
            if shift is not None:
                tmax = tmax + shift
            shifts.append(shift)
            m = tmax if m is None else jnp.maximum(m, tmax)
            yield
        out.append((m, shifts))

    def outputs(qi, slot, m, shifts):
        sbuf = s_ref.at[slot]
        acc = None
        for n in range(qi + 1):
            off = m if shifts[n] is None else m - shifts[n]
            p = jnp.exp2((sbuf[n * L:(n + 1) * L, :] - off).astype(BF16))
            pv = jnp.dot(vt_ref[:, n * L:(n + 1) * L], p,
                         preferred_element_type=F32)
            acc = pv if acc is None else acc + pv
            yield
        l = acc[HEAD_DIM:HEAD_DIM + 1, :]
        y = (acc[:HEAD_DIM, :] * (1.0 / l)).T
        z = z_ref[qi * L:(qi + 1) * L, :].astype(F32)
        o_ref[qi * L:(qi + 1) * L, :] = (y * _silu(z)).astype(o_ref.dtype)

    width = len(MOBA_GROUPS[0])
    pending = []
    for t, grp in enumerate(MOBA_GROUPS):
        outs = [[] for _ in grp]
        slots = [(t % 2) * width + i for i in range(width)]
        yield from _sweeps([scores(qi, slots[i], outs[i]) for i, qi in enumerate(grp)]
                           + [outputs(*args) for args in pending])
        pending = [(qi, slots[i]) + outs[i][0] for i, qi in enumerate(grp)]
    yield from _sweeps([outputs(*args) for args in pending])


def _retention_tables(S):
    C = RET_CHUNK
    dk = HEAD_DIM
    f32 = np.float32
    pos = np.arange(S, dtype=f32)
    theta = (1.0 / (f32(ROT_BASE) ** np.linspace(0.0, 1.0, dk // 2, dtype=f32))).astype(f32)
    ang = pos[:, None] * theta[None, :]
    cos = np.repeat(np.cos(ang), 2, axis=1)
    sin = np.repeat(np.sin(ang), 2, axis=1) * np.tile(np.array([-1.0, 1.0], f32), dk // 2)
    log_gamma = np.log(1.0 - 2.0 ** (-5.0 - np.arange(N_HEADS, dtype=f32))).astype(f32)
    i = np.arange(C)
    diff = i[:, None] - i[None, :]
    k_scale = f32(dk ** -0.5)
    decay = np.where(diff >= 0,
                     np.exp(np.maximum(diff, 0).astype(f32) * log_gamma[:, None, None]),
                     0.0) * k_scale
    zeta = np.exp((C - 1 - i).astype(f32)[None, :] * log_gamma[:, None]) * k_scale
    xi = np.exp((i + 1).astype(f32)[None, :] * log_gamma[:, None])
    zeta = np.broadcast_to(zeta[:, :, None], (N_HEADS, C, dk))
    xi = np.broadcast_to(xi[:, :, None], (N_HEADS, C, dk))
    chunk_decay = np.exp(C * log_gamma)
    rope = np.concatenate([cos, sin], axis=1)
    return tuple(jnp.asarray(np.ascontiguousarray(t, dtype=f32))
                 for t in (rope, decay, zeta, xi, chunk_decay))


def _ret_chains(cd_ref, q_ref, k_ref, v_ref, z_ref, cos_ref, sin_ref, dm_ref, zeta_ref, xi_ref,
                o_ref, r_ref, first):
    C = RET_CHUNK
    n_chunks = q_ref.shape[0] // C

    @pl.when(first)
    def _():
        r_ref[...] = jnp.zeros_like(r_ref)

    lane = lax.broadcasted_iota(jnp.int32, (C, HEAD_DIM), 1)
    even = (lane % 2) == 0

    def rot(t, cosv, sinv):
        partner = jnp.where(even, pltpu.roll(t, HEAD_DIM - 1, 1), pltpu.roll(t, 1, 1))
        return t * cosv + partner * sinv

    def head(h):
        cols = slice(h * HEAD_DIM, (h + 1) * HEAD_DIM)
        for c in range(n_chunks):
            rows = slice(c * C, (c + 1) * C)
            cosv = cos_ref[rows, :]
            sinv = sin_ref[rows, :]
            q = rot(q_ref[rows, cols].astype(F32), cosv, sinv)
            k = rot(k_ref[rows, cols].astype(F32), cosv, sinv)
            v = v_ref[rows, cols]
            qb = q.astype(BF16)
            yield
            inner = lax.dot_general(qb, k.astype(BF16), NT_DIMS,
                                    preferred_element_type=F32) * dm_ref[h]
            r_old = r_ref[h]
            kz = (k * zeta_ref[h]).astype(BF16)
            r_ref[h] = r_old * cd_ref[h] + lax.dot_general(kz, v, TN_DIMS,
                                                           preferred_element_type=F32)
            yield
            lhs = jnp.concatenate([inner.astype(BF16), (q * xi_ref[h]).astype(BF16)], axis=1)
            rhs = jnp.concatenate([v, r_old.astype(BF16)], axis=0)
            o = jnp.dot(lhs, rhs, preferred_element_type=F32)
            yield
            o = o * lax.rsqrt(jnp.mean(o * o, axis=-1, keepdims=True) + EPS)
            o_ref[rows, cols] = (o * _silu(z_ref[rows, cols].astype(F32))).astype(o_ref.dtype)
            yield

    return [head(h) for h in range(N_HEADS)]


def _attn_kernel(n_cast, rb_ref, qa_ref, ka_ref, va_ref, za_ref, tab_ref,
                 cd_ref, mixb_ref, rope_ref, dm_ref, zeta_ref, xi_ref, hx_ref, wg_ref, *refs):
    w32_refs = refs[:n_cast]
    ya_ref, yb_ref, gate_ref = refs[n_cast:n_cast + 3]
    w16_refs = refs[n_cast + 3:2 * n_cast + 3]
    vt_ref, s_ref, r_ref = refs[2 * n_cast + 3:]

    def gate_proj():
        tm, tn = gate_ref.shape
        for _ in range(GATE_DELAY):
            yield
        for r0 in range(0, tm, GATE_ROWS):
            for c0 in range(0, tn, GATE_COLS):
                acc = jnp.dot(hx_ref[r0:r0 + GATE_ROWS, :],
                              wg_ref[:, c0:c0 + GATE_COLS].astype(BF16),
                              preferred_element_type=F32)
                gate_ref[r0:r0 + GATE_ROWS, c0:c0 + GATE_COLS] = acc.astype(gate_ref.dtype)
                for _ in range(GATE_EVERY):
                    yield

    def casts():
        for w32, w16 in zip(w32_refs, w16_refs):
            w16[...] = w32[...].astype(BF16)
            yield

    qb_ref, kb_ref, vb_ref, zb_ref = (mixb_ref.at[:, c * D_A:(c + 1) * D_A] for c in range(4))
    cos_ref, sin_ref = (rope_ref.at[:, c * HEAD_DIM:(c + 1) * HEAD_DIM] for c in range(2))
    heads = _ret_chains(cd_ref, qb_ref, kb_ref, vb_ref, zb_ref, cos_ref, sin_ref, dm_ref, zeta_ref,
                        xi_ref, yb_ref, r_ref, first=pl.program_id(1) == 0)
    lanes = [_chain(heads[i::RET_LANES]) for i in range(RET_LANES)]
    moba = _moba_sweeps(rb_ref, qa_ref, ka_ref, va_ref, za_ref, tab_ref, ya_ref, vt_ref, s_ref)
    _round_robin([moba] + lanes + [casts(), gate_proj()])


def _attention(proj3, tabs, rel_bias, weights, hx, w_in, gate_col0, tm, tn):
    B, S, _ = proj3.shape
    L = MOBA_BLOCK
    C = RET_CHUNK
    dk = HEAD_DIM
    rows_b = S // N_HEADS
    assert sorted(qi for grp in MOBA_GROUPS for qi in grp) == list(range(S // L))
    assert rows_b % C == 0
    n_steps = B * N_HEADS
    gate_tiles = 2 * D_MODEL // tn
    assert (B * S // tm) * gate_tiles == n_steps
    rope, decay, zeta, xi, chunk_decay = _retention_tables(S)

    def gate_tile(b, h):
        step = b * N_HEADS + h
        return step // gate_tiles, step % gate_tiles

    def head_cols(col0):
        return pl.BlockSpec((None, S, HEAD_DIM), lambda b, h: (b, 0, col0 + h))

    def all_heads(col0):
        return pl.BlockSpec((None, rows_b, D_A), lambda b, h: (b, h, col0))

    def w_slice(w):
        rows = max(w.shape[0] // n_steps, BF16_SUBLANES)
        n_slices = w.shape[0] // rows
        assert rows * n_slices == w.shape[0] and n_slices <= n_steps, w.shape
        return pl.BlockSpec((rows, w.shape[1]),
                            lambda b, h: (jnp.minimum(b * N_HEADS + h, n_slices - 1), 0))

    head_tab = pl.BlockSpec((N_HEADS, C, dk), lambda b, h: (0, 0, 0))
    w_specs = [w_slice(w) for w in weights]
    smem = pl.BlockSpec(memory_space=pltpu.SMEM)
    outs = pl.pallas_call(
        functools.partial(_attn_kernel, len(weights)),
        name="attention",
        grid=(B, N_HEADS),
        in_specs=[
            smem,
            head_cols(COL_QA), head_cols(COL_KA), head_cols(COL_VA), head_cols(COL_ZA),
            pl.BlockSpec((None, 2, L, L), lambda b, h: (h, 0, 0, 0)),
            smem,
            pl.BlockSpec((None, rows_b, 4 * D_A), lambda b, h: (b, h, COL_MIXB)),
            pl.BlockSpec((rows_b, 2 * dk), lambda b, h: (h, 0)),
            head_tab, head_tab, head_tab,
            pl.BlockSpec((tm, hx.shape[1]), lambda b, h: (gate_tile(b, h)[0], 0)),
            pl.BlockSpec((w_in.shape[0], tn),
                         lambda b, h: (0, gate_col0 // tn + gate_tile(b, h)[1])),
            *w_specs,
        ],
        out_specs=[head_cols(0), all_heads(0), pl.BlockSpec((tm, tn), gate_tile), *w_specs],
        out_shape=[jax.ShapeDtypeStruct((B, S, D_A), BF16)] * 2
        + [jax.ShapeDtypeStruct((B * S, 2 * D_MODEL), BF16)]
        + [jax.ShapeDtypeStruct(w.shape, BF16) for w in weights],
        scratch_shapes=[
            pltpu.VMEM((HEAD_DIM + VT_PAD, S), BF16),
            pltpu.VMEM((2 * len(MOBA_GROUPS[0]), S, L), F32),
            pltpu.VMEM((N_HEADS, dk, dk), F32),
        ],
        compiler_params=pltpu.CompilerParams(
            dimension_semantics=("arbitrary", "arbitrary"),
            vmem_limit_bytes=VMEM_LIMIT_V7X),
    )(rel_bias, proj3, proj3, proj3, proj3, tabs,
      chunk_decay, proj3, rope, decay, zeta, xi, hx, w_in, *weights)
    return outs[0], outs[1], outs[2], outs[3:]


def _tail_kernel(ya_ref, yb_ref, ga_ref, gb_ref, x_ref, p_ref, wa_ref, wb_ref, wo_ref, wpg_ref,
                 wpp_ref, gple_ref, gfin_ref, o_ref):
    ua = jnp.dot(ya_ref[...], wa_ref[...], preferred_element_type=F32)
    ub = jnp.dot(yb_ref[...], wb_ref[...], preferred_element_type=F32)
    merged = (jax.nn.sigmoid(ga_ref[...].astype(F32)) * ua
              + jax.nn.sigmoid(gb_ref[...].astype(F32)) * ub)
    x1 = x_ref[...] + jnp.dot(merged.astype(BF16), wo_ref[...], preferred_element_type=F32)
    hp = _rms_norm(x1, gple_ref[...]).astype(BF16)
    gate = jax.nn.sigmoid(jnp.dot(hp, wpg_ref[...], preferred_element_type=F32))
    pe = jnp.dot(p_ref[...].astype(BF16), wpp_ref[...], preferred_element_type=F32)
    x2 = x1 + gate * pe
    o_ref[...] = _rms_norm(x2, gfin_ref[...])


def _tail(ya, yb, gates, x2d, p2d, wa, wb, wo, wpg, wpp, g_ple, g_final, tm):
    m, d = x2d.shape

    def const(shape):
        return pl.BlockSpec(shape, lambda i: (0, 0), pipeline_mode=pl.Buffered(1))

    return pl.pallas_call(
        _tail_kernel,
        name="tail",
        grid=(m // tm,),
        in_specs=[
            pl.BlockSpec((tm, D_A), lambda i: (i, 0)),
            pl.BlockSpec((tm, D_A), lambda i: (i, 0)),
            pl.BlockSpec((tm, d), lambda i: (i, 0)),
            pl.BlockSpec((tm, d), lambda i: (i, 1)),
            pl.BlockSpec((tm, d), lambda i: (i, 0)),
            pl.BlockSpec((tm, D_PLE), lambda i: (i, 0)),
            const((D_A, d)), const((D_A, d)), const((d, d)), const((d, d)), const((D_PLE, d)),
            const((1, d)), const((1, d)),
        ],
        out_specs=pl.BlockSpec((tm, d), lambda i: (i, 0)),
        out_shape=jax.ShapeDtypeStruct((m, d), F32),
        compiler_params=pltpu.CompilerParams(
            dimension_semantics=("parallel",),
            vmem_limit_bytes=VMEM_LIMIT_V7X),
    )(ya, yb, gates, gates, x2d, p2d, wa, wb, wo, wpg, wpp, g_ple, g_final)


def kernel(x, p, g_mix, w_in, w_a, w_b, w_out, g_ple, w_ple_gate, w_ple_proj, rel_bias, g_final):
    B, S, d = x.shape
    assert w_in.shape[0] == 1, "the fused tail applies the final norm: single-layer stacks only"
    tabs = _bias_tables(rel_bias)
    x2d = x.reshape(B * S, d)
    n_mix = 8 * D_A
    col_scale = jnp.asarray(
        np.where(np.arange(n_mix) < D_A, HEAD_DIM ** -0.5 * LOG2E, 1.0).astype(np.float32))
    proj, hx = _project(x2d, g_mix, w_in[0], col_scale[None, :], n_mix, PROJ_TM, PROJ_TN)
    proj3 = proj.reshape(B, S, n_mix)
    ya, yb, gates, (wa, wb, wo, wpg, wpp) = _attention(
        proj3, tabs, rel_bias, (w_a[0], w_b[0], w_out[0], w_ple_gate[0], w_ple_proj[0]),
        hx, w_in[0], n_mix, PROJ_TM, PROJ_TN)
    out = _tail(ya.reshape(B * S, D_A), yb.reshape(B * S, D_A), gates, x2d,
                p[0].reshape(B * S, D_PLE), wa, wb, wo, wpg, wpp,
                g_ple, g_final[None, :], TAIL_TM)
    return out.reshape(B, S, d)
```

```python
import functools
import math

import numpy as np
import jax
import jax.numpy as jnp
from jax import lax
from jax.experimental import pallas as pl
from jax.experimental.pallas import tpu as pltpu

D_MODEL = 2048
N_HEADS = 8
HEAD_DIM = 128
D_A = N_HEADS * HEAD_DIM
MOBA_BLOCK = 256
MOBA_TOPK = 3
N_BUCKETS = 32
MAX_DISTANCE = 128
RET_CHUNK = 128
ROT_BASE = 10000.0
D_PLE = 256
EPS = 1e-6

COL_QA, COL_KA, COL_VA, COL_ZA = (i * N_HEADS for i in range(4))
COL_MIXB = 1

VMEM_LIMIT_V7X = 56 * 1024 * 1024
PROJ_TM = 1024
PROJ_TN = 1024
TAIL_TM = 256
BF16_SUBLANES = 16
VT_PAD = BF16_SUBLANES
MOBA_GROUPS = ((7, 0), (6, 1), (5, 2), (4, 3))
RET_LANES = 2
X_PARTS = 4
GATE_ROWS = 1024
GATE_COLS = 256
GATE_EVERY = 8
GATE_DELAY = 2

F32 = jnp.float32
BF16 = jnp.bfloat16
NEG_INF = float("-inf")
LOG2E = math.log2(math.e)
NT_DIMS = (((1,), (1,)), ((), ()))
TN_DIMS = (((0,), (0,)), ((), ()))


def _rms_norm(x, g):
    return x * lax.rsqrt(jnp.mean(x * x, axis=-1, keepdims=True) + EPS) * g


def _silu(z):
    return z * jax.nn.sigmoid(z)


def _sweeps(gens):
    gens = list(gens)
    while gens:
        for g in list(gens):
            try:
                next(g)
            except StopIteration:
                gens.remove(g)
        yield


def _round_robin(gens):
    for _ in _sweeps(gens):
        pass


def _chain(gens):
    for g in gens:
        yield from g


def _proj_kernel(*refs):
    x_refs, (g_ref, w_ref, cs_ref, o_ref, h_ref) = refs[:X_PARTS], refs[X_PARTS:]

    rows = h_ref.shape[0] // X_PARTS

    @pl.when(pl.program_id(1) == 0)
    def _():
        w = w_ref[...].astype(BF16)
        for q, x_ref in enumerate(x_refs):
            part = slice(q * rows, (q + 1) * rows)
            hq = _rms_norm(x_ref[...], g_ref[...]).astype(BF16)
            h_ref[part, :] = hq
            acc = jnp.dot(hq, w, preferred_element_type=F32)
            o_ref[part, :] = (acc * cs_ref[...]).astype(o_ref.dtype)

    @pl.when(pl.program_id(1) != 0)
    def _():
        acc = jnp.dot(h_ref[...], w_ref[...].astype(BF16), preferred_element_type=F32)
        o_ref[...] = (acc * cs_ref[...]).astype(o_ref.dtype)


def _project(x2d, g, w, col_scale, n, tm, tn):
    m, d = x2d.shape
    n_i, n_j = m // tm, n // tn
    assert n_j >= X_PARTS

    def x_part(q):
        def index(i, j):
            nxt = jnp.minimum(i + (j >= n_j - X_PARTS + q).astype(jnp.int32), n_i - 1)
            return (nxt * X_PARTS + q, 0)
        return pl.BlockSpec((tm // X_PARTS, d), index)

    return pl.pallas_call(
        _proj_kernel,
        name="proj",
        grid=(n_i, n_j),
        in_specs=[
            *[x_part(q) for q in range(X_PARTS)],
            pl.BlockSpec((1, d), lambda i, j: (0, 0)),
            pl.BlockSpec((d, tn), lambda i, j: (0, j)),
            pl.BlockSpec((1, tn), lambda i, j: (0, j)),
        ],
        out_specs=[pl.BlockSpec((tm, tn), lambda i, j: (i, j)),
                   pl.BlockSpec((tm, d), lambda i, j: (i, 0))],
        out_shape=[jax.ShapeDtypeStruct((m, n), BF16), jax.ShapeDtypeStruct((m, d), BF16)],
        compiler_params=pltpu.CompilerParams(
            dimension_semantics=("arbitrary", "arbitrary"),
            vmem_limit_bytes=VMEM_LIMIT_V7X),
    )(*[x2d] * X_PARTS, g, w, col_scale)


def _t5_bucket_np(n):
    max_exact = N_BUCKETS // 2
    nf = np.maximum(n, 1).astype(np.float32)
    val = (np.log(nf / max_exact) / np.float32(math.log(MAX_DISTANCE / max_exact))
           * (N_BUCKETS - max_exact))
    large = np.minimum(max_exact + val.astype(np.int32), N_BUCKETS - 1)
    return np.where(n < max_exact, n, large).astype(np.int32)


def _bucket_row():
    return _t5_bucket_np(np.arange(2 * MOBA_BLOCK))[None, :].astype(np.int32)


def _bias_kernel(rb_ref, bkt_ref, o_ref):
    L = MOBA_BLOCK
    bkt = bkt_ref[...]
    key = lax.broadcasted_iota(jnp.int32, (L, L), 0)
    qry = lax.broadcasted_iota(jnp.int32, (L, L), 1)
    for h in range(N_HEADS):
        f = jnp.zeros(bkt.shape, F32)
        for b in range(N_BUCKETS):
            f = jnp.where(bkt == b, rb_ref[b, h] * LOG2E, f)
        both = pltpu.roll(jnp.broadcast_to(f, (L, 2 * L)), 0, 1, stride=1, stride_axis=0)
        o_ref[h, 0] = jnp.where(qry >= key, both[:, :L], NEG_INF)
        o_ref[h, 1] = both[:, L:]


def _bias_tables(rel_bias):
    bkt = jnp.asarray(_bucket_row())
    L = MOBA_BLOCK
    return pl.pallas_call(
        _bias_kernel,
        name="t5_bias",
        in_specs=[
            pl.BlockSpec(memory_space=pltpu.SMEM),
            pl.BlockSpec(memory_space=pltpu.VMEM),
        ],
        out_specs=pl.BlockSpec(memory_space=pltpu.VMEM),
        out_shape=jax.ShapeDtypeStruct((N_HEADS, 2, L, L), F32),
    )(rel_bias, bkt)


def _moba_sweeps(rb_ref, q_ref, k_ref, v_ref, z_ref, tab_ref, o_ref, vt_ref, s_ref):
    L = MOBA_BLOCK
    nb = k_ref.shape[0] // L
    h = pl.program_id(1)

    kf = k_ref[...].astype(F32).reshape(nb, L, HEAD_DIM)
    km = jnp.mean(kf, axis=1)
    km_hi = km.astype(BF16)
    km_lo = (km - km_hi.astype(F32)).astype(BF16)
    km2 = jnp.concatenate([km_hi, km_lo], axis=0)
    vt_ref[:HEAD_DIM, :] = v_ref[...].T
    ones_row = lax.broadcasted_iota(jnp.int32, (VT_PAD, vt_ref.shape[1]), 0) == 0
    vt_ref[HEAD_DIM:, :] = jnp.where(ones_row, 1.0, 0.0).astype(BF16)
    far_bias = rb_ref[N_BUCKETS - 1, h] * LOG2E
    row = lax.broadcasted_iota(jnp.int32, (nb, L), 0)

    def scores(qi, slot, out):
        q = q_ref[qi * L:(qi + 1) * L, :]
        sel_add = None
        if qi > MOBA_TOPK:
            g2 = lax.dot_general(km2, q, NT_DIMS, preferred_element_type=F32)
            g = g2[:nb] + g2[nb:]
            cnt = jnp.zeros((nb, L), jnp.int32)
            for mm in range(qi):
                gm = g[mm:mm + 1, :]
                ge = jnp.where(gm >= g, 1, 0)
                gt = jnp.where(gm > g, 1, 0)
                cnt = cnt + jnp.where(row > mm, ge, gt)
            sel_add = jnp.where(cnt < MOBA_TOPK, 0.0, NEG_INF)

        sbuf = s_ref.at[slot]
        shifts = []
        m = None
        for n in range(qi + 1):
            s = lax.dot_general(k_ref[n * L:(n + 1) * L, :], q, NT_DIMS,
                                preferred_element_type=F32)
            shift = None
            if n == qi:
                s = s + tab_ref[0]
            else:
                if n == qi - 1:
                    s = s + tab_ref[1]
                if sel_add is not None:
                    shift = sel_add[n:n + 1, :]
                if n < qi - 1:
                    shift = far_bias if shift is None else shift + far_bias
            sbuf[n * L:(n + 1) * L, :] = s
            tmax = jnp.max(s, axis=0, keepdims=True)
            if shift is not None:
                tmax = tmax + shift
            shifts.append(shift)
            m = tmax if m is None else jnp.maximum(m, tmax)
            yield
        out.append((m, shifts))

    def outputs(qi, slot, m, shifts):
        sbuf = s_ref.at[slot]
        acc = None
        for n in range(qi + 1):
            off = m if shifts[n] is None else m - shifts[n]
            p = jnp.exp2((sbuf[n * L:(n + 1) * L, :] - off).astype(BF16))
            pv = jnp.dot(vt_ref[:, n * L:(n + 1) * L], p,
                         preferred_element_type=F32)
            acc = pv if acc is None else acc + pv
            yield
        l = acc[HEAD_DIM:HEAD_DIM + 1, :]
        y = (acc[:HEAD_DIM, :] * (1.0 / l)).T
        z = z_ref[qi * L:(qi + 1) * L, :].astype(F32)
        o_ref[qi * L:(qi + 1) * L, :] = (y * _silu(z)).astype(o_ref.dtype)

    width = len(MOBA_GROUPS[0])
    pending = []
    for t, grp in enumerate(MOBA_GROUPS):
        outs = [[] for _ in grp]
        slots = [(t % 2) * width + i for i in range(width)]
        yield from _sweeps([scores(qi, slots[i], outs[i]) for i, qi in enumerate(grp)]
                           + [outputs(*args) for args in pending])
        pending = [(qi, slots[i]) + outs[i][0] for i, qi in enumerate(grp)]
    yield from _sweeps([outputs(*args) for args in pending])


def _retention_tables(S):
    C = RET_CHUNK
    dk = HEAD_DIM
    f32 = np.float32
    pos = np.arange(S, dtype=f32)
    theta = (1.0 / (f32(ROT_BASE) ** np.linspace(0.0, 1.0, dk // 2, dtype=f32))).astype(f32)
    ang = pos[:, None] * theta[None, :]
    cos = np.repeat(np.cos(ang), 2, axis=1)
    sin = np.repeat(np.sin(ang), 2, axis=1) * np.tile(np.array([-1.0, 1.0], f32), dk // 2)
    log_gamma = np.log(1.0 - 2.0 ** (-5.0 - np.arange(N_HEADS, dtype=f32))).astype(f32)
    i = np.arange(C)
    diff = i[:, None] - i[None, :]
    k_scale = f32(dk ** -0.5)
    decay = np.where(diff >= 0,
                     np.exp(np.maximum(diff, 0).astype(f32) * log_gamma[:, None, None]),
                     0.0) * k_scale
    zeta = np.exp((C - 1 - i).astype(f32)[None, :] * log_gamma[:, None]) * k_scale
    xi = np.exp((i + 1).astype(f32)[None, :] * log_gamma[:, None])
    zeta = np.broadcast_to(zeta[:, :, None], (N_HEADS, C, dk))
    xi = np.broadcast_to(xi[:, :, None], (N_HEADS, C, dk))
    chunk_decay = np.exp(C * log_gamma)
    rope = np.concatenate([cos, sin], axis=1)
    return tuple(jnp.asarray(np.ascontiguousarray(t, dtype=f32))
                 for t in (rope, decay, zeta, xi, chunk_decay))


def _ret_chains(cd_ref, q_ref, k_ref, v_ref, z_ref, cos_ref, sin_ref, dm_ref, zeta_ref, xi_ref,
                o_ref, r_ref, first):
    C = RET_CHUNK
    n_chunks = q_ref.shape[0] // C

    @pl.when(first)
    def _():
        r_ref[...] = jnp.zeros_like(r_ref)

    lane = lax.broadcasted_iota(jnp.int32, (C, HEAD_DIM), 1)
    even = (lane % 2) == 0

    def rot(t, cosv, sinv):
        partner = jnp.where(even, pltpu.roll(t, HEAD_DIM - 1, 1), pltpu.roll(t, 1, 1))
        return t * cosv + partner * sinv

    def head(h):
        cols = slice(h * HEAD_DIM, (h + 1) * HEAD_DIM)
        for c in range(n_chunks):
            rows = slice(c * C, (c + 1) * C)
            cosv = cos_ref[rows, :]
            sinv = sin_ref[rows, :]
            q = rot(q_ref[rows, cols].astype(F32), cosv, sinv)
            k = rot(k_ref[rows, cols].astype(F32), cosv, sinv)
            v = v_ref[rows, cols]
            qb = q.astype(BF16)
            yield
            inner = lax.dot_general(qb, k.astype(BF16), NT_DIMS,
                                    preferred_element_type=F32) * dm_ref[h]
            r_old = r_ref[h]
            kz = (k * zeta_ref[h]).astype(BF16)
            r_ref[h] = r_old * cd_ref[h] + lax.dot_general(kz, v, TN_DIMS,
                                                           preferred_element_type=F32)
            yield
            lhs = jnp.concatenate([inner.astype(BF16), (q * xi_ref[h]).astype(BF16)], axis=1)
            rhs = jnp.concatenate([v, r_old.astype(BF16)], axis=0)
            o = jnp.dot(lhs, rhs, preferred_element_type=F32)
            yield
            o = o * lax.rsqrt(jnp.mean(o * o, axis=-1, keepdims=True) + EPS)
            o_ref[rows, cols] = (o * _silu(z_ref[rows, cols].astype(F32))).astype(o_ref.dtype)
            yield

    return [head(h) for h in range(N_HEADS)]


def _attn_kernel(n_cast, rb_ref, qa_ref, ka_ref, va_ref, za_ref, tab_ref,
                 cd_ref, mixb_ref, rope_ref, dm_ref, zeta_ref, xi_ref, hx_ref, wg_ref, *refs):
    w32_refs = refs[:n_cast]
    ya_ref, yb_ref, gate_ref = refs[n_cast:n_cast + 3]
    w16_refs = refs[n_cast + 3:2 * n_cast + 3]
    vt_ref, s_ref, r_ref = refs[2 * n_cast + 3:]

    def gate_proj():
        tm, tn = gate_ref.shape
        for _ in range(GATE_DELAY):
            yield
        for r0 in range(0, tm, GATE_ROWS):
            for c0 in range(0, tn, GATE_COLS):
                acc = jnp.dot(hx_ref[r0:r0 + GATE_ROWS, :],
                              wg_ref[:, c0:c0 + GATE_COLS].astype(BF16),
                              preferred_element_type=F32)
                gate_ref[r0:r0 + GATE_ROWS, c0:c0 + GATE_COLS] = acc.astype(gate_ref.dtype)
                for _ in range(GATE_EVERY):
                    yield

    def casts():
        for w32, w16 in zip(w32_refs, w16_refs):
            w16[...] = w32[...].astype(BF16)
            yield

    qb_ref, kb_ref, vb_ref, zb_ref = (mixb_ref.at[:, c * D_A:(c + 1) * D_A] for c in range(4))
    cos_ref, sin_ref = (rope_ref.at[:, c * HEAD_DIM:(c + 1) * HEAD_DIM] for c in range(2))
    heads = _ret_chains(cd_ref, qb_ref, kb_ref, vb_ref, zb_ref, cos_ref, sin_ref, dm_ref, zeta_ref,
                        xi_ref, yb_ref, r_ref, first=pl.program_id(1) == 0)
    lanes = [_chain(heads[i::RET_LANES]) for i in range(RET_LANES)]
    moba = _moba_sweeps(rb_ref, qa_ref, ka_ref, va_ref, za_ref, tab_ref, ya_ref, vt_ref, s_ref)
    _round_robin([moba] + lanes + [casts(), gate_proj()])


def _attention(proj3, tabs, rel_bias, weights, hx, w_in, gate_col0, tm, tn):
    B, S, _ = proj3.shape
    L = MOBA_BLOCK
    C = RET_CHUNK
    dk = HEAD_DIM
    rows_b = S // N_HEADS
    assert sorted(qi for grp in MOBA_GROUPS for qi in grp) == list(range(S // L))
    assert rows_b % C == 0
    n_steps = B * N_HEADS
    gate_tiles = 2 * D_MODEL // tn
    assert (B * S // tm) * gate_tiles == n_steps
    rope, decay, zeta, xi, chunk_decay = _retention_tables(S)

    def gate_tile(b, h):
        step = b * N_HEADS + h
        return step // gate_tiles, step % gate_tiles

    def head_cols(col0):
        return pl.BlockSpec((None, S, HEAD_DIM), lambda b, h: (b, 0, col0 + h))

    def all_heads(col0):
        return pl.BlockSpec((None, rows_b, D_A), lambda b, h: (b, h, col0))

    def w_slice(w):
        rows = max(w.shape[0] // n_steps, BF16_SUBLANES)
        n_slices = w.shape[0] // rows
        assert rows * n_slices == w.shape[0] and n_slices <= n_steps, w.shape
        return pl.BlockSpec((rows, w.shape[1]),
                            lambda b, h: (jnp.minimum(b * N_HEADS + h, n_slices - 1), 0))

    head_tab = pl.BlockSpec((N_HEADS, C, dk), lambda b, h: (0, 0, 0))
    w_specs = [w_slice(w) for w in weights]
    smem = pl.BlockSpec(memory_space=pltpu.SMEM)
    outs = pl.pallas_call(
        functools.partial(_attn_kernel, len(weights)),
        name="attention",
        grid=(B, N_HEADS),
        in_specs=[
            smem,
            head_cols(COL_QA), head_cols(COL_KA), head_cols(COL_VA), head_cols(COL_ZA),
            pl.BlockSpec((None, 2, L, L), lambda b, h: (h, 0, 0, 0)),
            smem,
            pl.BlockSpec((None, rows_b, 4 * D_A), lambda b, h: (b, h, COL_MIXB)),
            pl.BlockSpec((rows_b, 2 * dk), lambda b, h: (h, 0)),
            head_tab, head_tab, head_tab,
            pl.BlockSpec((tm, hx.shape[1]), lambda b, h: (gate_tile(b, h)[0], 0)),
            pl.BlockSpec((w_in.shape[0], tn),
                         lambda b, h: (0, gate_col0 // tn + gate_tile(b, h)[1])),
            *w_specs,
        ],
        out_specs=[head_cols(0), all_heads(0), pl.BlockSpec((tm, tn), gate_tile), *w_specs],
        out_shape=[jax.ShapeDtypeStruct((B, S, D_A), BF16)] * 2
        + [jax.ShapeDtypeStruct((B * S, 2 * D_MODEL), BF16)]
        + [jax.ShapeDtypeStruct(w.shape, BF16) for w in weights],
        scratch_shapes=[
            pltpu.VMEM((HEAD_DIM + VT_PAD, S), BF16),
            pltpu.VMEM((2 * len(MOBA_GROUPS[0]), S, L), F32),
            pltpu.VMEM((N_HEADS, dk, dk), F32),
        ],
        compiler_params=pltpu.CompilerParams(
            dimension_semantics=("arbitrary", "arbitrary"),
            vmem_limit_bytes=VMEM_LIMIT_V7X),
    )(rel_bias, proj3, proj3, proj3, proj3, tabs,
      chunk_decay, proj3, rope, decay, zeta, xi, hx, w_in, *weights)
    return outs[0], outs[1], outs[2], outs[3:]


def _tail_kernel(ya_ref, yb_ref, ga_ref, gb_ref, x_ref, p_ref, wa_ref, wb_ref, wo_ref, wpg_ref,
                 wpp_ref, gple_ref, gfin_ref, o_ref, x1_ref, pe_ref):
    s = pl.program_id(0)
    last = pl.num_programs(0) - 1

    def first_half():
        ua = jnp.dot(ya_ref[...], wa_ref[...], preferred_element_type=F32)
        yield
        ub = jnp.dot(yb_ref[...], wb_ref[...], preferred_element_type=F32)
        merged = (jax.nn.sigmoid(ga_ref[...].astype(F32)) * ua
                  + jax.nn.sigmoid(gb_ref[...].astype(F32)) * ub)
        yield
        x1 = x_ref[...] + jnp.dot(merged.astype(BF16), wo_ref[...], preferred_element_type=F32)
        pe = jnp.dot(p_ref[...].astype(BF16), wpp_ref[...], preferred_element_type=F32)
        yield
        x1_ref[...] = x1
        pe_ref[...] = pe

    def second_half(x1, pe):
        hp = _rms_norm(x1, gple_ref[...]).astype(BF16)
        yield
        gate = jax.nn.sigmoid(jnp.dot(hp, wpg_ref[...], preferred_element_type=F32))
        x2 = x1 + gate * pe
        yield
        o_ref[...] = _rms_norm(x2, gfin_ref[...])

    @pl.when(s == 0)
    def _():
        _round_robin([first_half()])

    @pl.when((s > 0) & (s < last))
    def _():
        _round_robin([second_half(x1_ref[...], pe_ref[...]), first_half()])

    @pl.when(s == last)
    def _():
        _round_robin([second_half(x1_ref[...], pe_ref[...])])


def _tail(ya, yb, gates, x2d, p2d, wa, wb, wo, wpg, wpp, g_ple, g_final, tm):
    m, d = x2d.shape
    n_t = m // tm

    def const(shape):
        return pl.BlockSpec(shape, lambda s: (0, 0), pipeline_mode=pl.Buffered(1))

    def first(s):
        return jnp.minimum(s, n_t - 1)

    def second(s):
        return jnp.maximum(s - 1, 0)

    return pl.pallas_call(
        _tail_kernel,
        name="tail",
        grid=(n_t + 1,),
        in_specs=[
            pl.BlockSpec((tm, D_A), lambda s: (first(s), 0)),
            pl.BlockSpec((tm, D_A), lambda s: (first(s), 0)),
            pl.BlockSpec((tm, d), lambda s: (first(s), 0)),
            pl.BlockSpec((tm, d), lambda s: (first(s), 1)),
            pl.BlockSpec((tm, d), lambda s: (first(s), 0)),
            pl.BlockSpec((tm, D_PLE), lambda s: (first(s), 0)),
            const((D_A, d)), const((D_A, d)), const((d, d)), const((d, d)), const((D_PLE, d)),
            const((1, d)), const((1, d)),
        ],
        out_specs=pl.BlockSpec((tm, d), lambda s: (second(s), 0)),
        out_shape=jax.ShapeDtypeStruct((m, d), F32),
        scratch_shapes=[pltpu.VMEM((tm, d), F32), pltpu.VMEM((tm, d), F32)],
        compiler_params=pltpu.CompilerParams(
            dimension_semantics=("arbitrary",),
            vmem_limit_bytes=VMEM_LIMIT_V7X),
    )(ya, yb, gates, gates, x2d, p2d, wa, wb, wo, wpg, wpp, g_ple, g_final)


def kernel(x, p, g_mix, w_in, w_a, w_b, w_out, g_ple, w_ple_gate, w_ple_proj, rel_bias, g_final):
    B, S, d = x.shape
    assert w_in.shape[0] == 1, "the fused tail applies the final norm: single-layer stacks only"
    tabs = _bias_tables(rel_bias)
    x2d = x.reshape(B * S, d)
    n_mix = 8 * D_A
    col_scale = jnp.asarray(
        np.where(np.arange(n_mix) < D_A, HEAD_DIM ** -0.5 * LOG2E, 1.0).astype(np.float32))
    proj, hx = _project(x2d, g_mix, w_in[0], col_scale[None, :], n_mix, PROJ_TM, PROJ_TN)
    proj3 = proj.reshape(B, S, n_mix)
    ya, yb, gates, (wa, wb, wo, wpg, wpp) = _attention(
        proj3, tabs, rel_bias, (w_a[0], w_b[0], w_out[0], w_ple_gate[0], w_ple_proj[0]),
        hx, w_in[0], n_mix, PROJ_TM, PROJ_TN)
    out = _tail(ya.reshape(B * S, D_A), yb.reshape(B * S, D_A), gates, x2d,
                p[0].reshape(B * S, D_PLE), wa, wb, wo, wpg, wpp,
                g_ple, g_final[None, :], TAIL_TM)
    return out.reshape(B, S, d)
```

```python
import functools
import math

import numpy as np
import jax
import jax.numpy as jnp
from jax import lax
from jax.experimental import pallas as pl
from jax.experimental.pallas import tpu as pltpu

D_MODEL = 2048
N_HEADS = 8
HEAD_DIM = 128
D_A = N_HEADS * HEAD_DIM
MOBA_BLOCK = 256
MOBA_TOPK = 3
N_BUCKETS = 32
MAX_DISTANCE = 128
RET_CHUNK = 128
ROT_BASE = 10000.0
D_PLE = 256
EPS = 1e-6

COL_QA, COL_KA, COL_VA, COL_ZA = (i * N_HEADS for i in range(4))
COL_MIXB = 1

VMEM_LIMIT_V7X = 56 * 1024 * 1024
PROJ_TM = 1024
PROJ_TN = 1024
TAIL_TM = 256
BF16_SUBLANES = 16
VT_PAD = BF16_SUBLANES
MOBA_GROUPS = ((7, 0), (6, 1), (5, 2), (4, 3))
RET_LANES = 2
RET_DELAY = 5
X_PARTS = 4
GATE_ROWS = 1024
GATE_COLS = 256
GATE_EVERY = 8
GATE_DELAY = 2

F32 = jnp.float32
BF16 = jnp.bfloat16
NEG_INF = float("-inf")
LOG2E = math.log2(math.e)
NT_DIMS = (((1,), (1,)), ((), ()))
TN_DIMS = (((0,), (0,)), ((), ()))


def _rms_norm(x, g):
    return x * lax.rsqrt(jnp.mean(x * x, axis=-1, keepdims=True) + EPS) * g


def _silu(z):
    return z * jax.nn.sigmoid(z)


def _sweeps(gens):
    gens = list(gens)
    while gens:
        for g in list(gens):
            try:
                next(g)
            except StopIteration:
                gens.remove(g)
        yield


def _round_robin(gens):
    for _ in _sweeps(gens):
        pass


def _chain(gens):
    for g in gens:
        yield from g


def _delayed(gen, sweeps):
    for _ in range(sweeps):
        yield
    yield from gen


def _proj_kernel(*refs):
    x_refs, (g_ref, w_ref, cs_ref, o_ref, h_ref) = refs[:X_PARTS], refs[X_PARTS:]

    rows = h_ref.shape[0] // X_PARTS

    @pl.when(pl.program_id(1) == 0)
    def _():
        w = w_ref[...].astype(BF16)
        for q, x_ref in enumerate(x_refs):
            part = slice(q * rows, (q + 1) * rows)
            hq = _rms_norm(x_ref[...], g_ref[...]).astype(BF16)
            h_ref[part, :] = hq
            acc = jnp.dot(hq, w, preferred_element_type=F32)
            o_ref[part, :] = (acc * cs_ref[...]).astype(o_ref.dtype)

    @pl.when(pl.program_id(1) != 0)
    def _():
        acc = jnp.dot(h_ref[...], w_ref[...].astype(BF16), preferred_element_type=F32)
        o_ref[...] = (acc * cs_ref[...]).astype(o_ref.dtype)


def _project(x2d, g, w, col_scale, n, tm, tn):
    m, d = x2d.shape
    n_i, n_j = m // tm, n // tn
    assert n_j >= X_PARTS

    def x_part(q):
        def index(i, j):
            nxt = jnp.minimum(i + (j >= n_j - X_PARTS + q).astype(jnp.int32), n_i - 1)
            return (nxt * X_PARTS + q, 0)
        return pl.BlockSpec((tm // X_PARTS, d), index)

    return pl.pallas_call(
        _proj_kernel,
        name="proj",
        grid=(n_i, n_j),
        in_specs=[
            *[x_part(q) for q in range(X_PARTS)],
            pl.BlockSpec((1, d), lambda i, j: (0, 0)),
            pl.BlockSpec((d, tn), lambda i, j: (0, j)),
            pl.BlockSpec((1, tn), lambda i, j: (0, j)),
        ],
        out_specs=[pl.BlockSpec((tm, tn), lambda i, j: (i, j)),
                   pl.BlockSpec((tm, d), lambda i, j: (i, 0))],
        out_shape=[jax.ShapeDtypeStruct((m, n), BF16), jax.ShapeDtypeStruct((m, d), BF16)],
        compiler_params=pltpu.CompilerParams(
            dimension_semantics=("arbitrary", "arbitrary"),
            vmem_limit_bytes=VMEM_LIMIT_V7X),
    )(*[x2d] * X_PARTS, g, w, col_scale)


def _t5_bucket_np(n):
    max_exact = N_BUCKETS // 2
    nf = np.maximum(n, 1).astype(np.float32)
    val = (np.log(nf / max_exact) / np.float32(math.log(MAX_DISTANCE / max_exact))
           * (N_BUCKETS - max_exact))
    large = np.minimum(max_exact + val.astype(np.int32), N_BUCKETS - 1)
    return np.where(n < max_exact, n, large).astype(np.int32)


def _bucket_row():
    return _t5_bucket_np(np.arange(2 * MOBA_BLOCK))[None, :].astype(np.int32)


def _bias_kernel(rb_ref, bkt_ref, o_ref):
    L = MOBA_BLOCK
    bkt = bkt_ref[...]
    key = lax.broadcasted_iota(jnp.int32, (L, L), 0)
    qry = lax.broadcasted_iota(jnp.int32, (L, L), 1)
    for h in range(N_HEADS):
        f = jnp.zeros(bkt.shape, F32)
        for b in range(N_BUCKETS):
            f = jnp.where(bkt == b, rb_ref[b, h] * LOG2E, f)
        both = pltpu.roll(jnp.broadcast_to(f, (L, 2 * L)), 0, 1, stride=1, stride_axis=0)
        o_ref[h, 0] = jnp.where(qry >= key, both[:, :L], NEG_INF)
        o_ref[h, 1] = both[:, L:]


def _bias_tables(rel_bias):
    bkt = jnp.asarray(_bucket_row())
    L = MOBA_BLOCK
    return pl.pallas_call(
        _bias_kernel,
        name="t5_bias",
        in_specs=[
            pl.BlockSpec(memory_space=pltpu.SMEM),
            pl.BlockSpec(memory_space=pltpu.VMEM),
        ],
        out_specs=pl.BlockSpec(memory_space=pltpu.VMEM),
        out_shape=jax.ShapeDtypeStruct((N_HEADS, 2, L, L), F32),
    )(rel_bias, bkt)


def _moba_sweeps(rb_ref, q_ref, k_ref, v_ref, z_ref, tab_ref, o_ref, vt_ref, s_ref):
    L = MOBA_BLOCK
    nb = k_ref.shape[0] // L
    h = pl.program_id(1)

    kf = k_ref[...].astype(F32).reshape(nb, L, HEAD_DIM)
    km = jnp.mean(kf, axis=1)
    km_hi = km.astype(BF16)
    km_lo = (km - km_hi.astype(F32)).astype(BF16)
    km2 = jnp.concatenate([km_hi, km_lo], axis=0)
    vt_ref[:HEAD_DIM, :] = v_ref[...].T
    ones_row = lax.broadcasted_iota(jnp.int32, (VT_PAD, vt_ref.shape[1]), 0) == 0
    vt_ref[HEAD_DIM:, :] = jnp.where(ones_row, 1.0, 0.0).astype(BF16)
    far_bias = rb_ref[N_BUCKETS - 1, h] * LOG2E
    row = lax.broadcasted_iota(jnp.int32, (nb, L), 0)

    def scores(qi, slot, out):
        q = q_ref[qi * L:(qi + 1) * L, :]
        sel_add = None
        if qi > MOBA_TOPK:
            g2 = lax.dot_general(km2, q, NT_DIMS, preferred_element_type=F32)
            g = g2[:nb] + g2[nb:]
            cnt = jnp.zeros((nb, L), jnp.int32)
            for mm in range(qi):
                gm = g[mm:mm + 1, :]
                ge = jnp.where(gm >= g, 1, 0)
                gt = jnp.where(gm > g, 1, 0)
                cnt = cnt + jnp.where(row > mm, ge, gt)
            sel_add = jnp.where(cnt < MOBA_TOPK, 0.0, NEG_INF)

        sbuf = s_ref.at[slot]
        shifts = []
        m = None
        for n in range(qi + 1):
            s = lax.dot_general(k_ref[n * L:(n + 1) * L, :], q, NT_DIMS,
                                preferred_element_type=F32)
            shift = None
            if n == qi:
                s = s + tab_ref[0]
            else:
                if n == qi - 1:
                    s = s + tab_ref[1]
                if sel_add is not None:
                    shift = sel_add[n:n + 1, :]
                if n < qi - 1:
                    shift = far_bias if shift is None else shift + far_bias
            sbuf[n * L:(n + 1) * L, :] = s
            tmax = jnp.max(s, axis=0, keepdims=True)
            if shift is not None:
                tmax = tmax + shift
            shifts.append(shift)
            m = tmax if m is None else jnp.maximum(m, tmax)
            yield
        out.append((m, shifts))

    def outputs(qi, slot, m, shifts):
        sbuf = s_ref.at[slot]
        acc = None
        for n in range(qi + 1):
            off = m if shifts[n] is None else m - shifts[n]
            p = jnp.exp2((sbuf[n * L:(n + 1) * L, :] - off).astype(BF16))
            pv = jnp.dot(vt_ref[:, n * L:(n + 1) * L], p,
                         preferred_element_type=F32)
            acc = pv if acc is None else acc + pv
            yield
        l = acc[HEAD_DIM:HEAD_DIM + 1, :]
        y = (acc[:HEAD_DIM, :] * (1.0 / l)).T
        z = z_ref[qi * L:(qi + 1) * L, :].astype(F32)
        o_ref[qi * L:(qi + 1) * L, :] = (y * _silu(z)).astype(o_ref.dtype)

    width = len(MOBA_GROUPS[0])
    pending = []
    for t, grp in enumerate(MOBA_GROUPS):
        outs = [[] for _ in grp]
        slots = [(t % 2) * width + i for i in range(width)]
        yield from _sweeps([scores(qi, slots[i], outs[i]) for i, qi in enumerate(grp)]
                           + [outputs(*args) for args in pending])
        pending = [(qi, slots[i]) + outs[i][0] for i, qi in enumerate(grp)]
    yield from _sweeps([outputs(*args) for args in pending])


def _retention_tables(S):
    C = RET_CHUNK
    dk = HEAD_DIM
    f32 = np.float32
    pos = np.arange(S, dtype=f32)
    theta = (1.0 / (f32(ROT_BASE) ** np.linspace(0.0, 1.0, dk // 2, dtype=f32))).astype(f32)
    ang = pos[:, None] * theta[None, :]
    cos = np.repeat(np.cos(ang), 2, axis=1)
    sin = np.repeat(np.sin(ang), 2, axis=1) * np.tile(np.array([-1.0, 1.0], f32), dk // 2)
    log_gamma = np.log(1.0 - 2.0 ** (-5.0 - np.arange(N_HEADS, dtype=f32))).astype(f32)
    i = np.arange(C)
    diff = i[:, None] - i[None, :]
    k_scale = f32(dk ** -0.5)
    decay = np.where(diff >= 0,
                     np.exp(np.maximum(diff, 0).astype(f32) * log_gamma[:, None, None]),
                     0.0) * k_scale
    zeta = np.exp((C - 1 - i).astype(f32)[None, :] * log_gamma[:, None]) * k_scale
    xi = np.exp((i + 1).astype(f32)[None, :] * log_gamma[:, None])
    zeta = np.broadcast_to(zeta[:, :, None], (N_HEADS, C, dk))
    xi = np.broadcast_to(xi[:, :, None], (N_HEADS, C, dk))
    chunk_decay = np.exp(C * log_gamma)
    rope = np.concatenate([cos, sin], axis=1)
    return tuple(jnp.asarray(np.ascontiguousarray(t, dtype=f32))
                 for t in (rope, decay, zeta, xi, chunk_decay))


def _ret_chains(cd_ref, q_ref, k_ref, v_ref, z_ref, cos_ref, sin_ref, dm_ref, zeta_ref, xi_ref,
                o_ref, r_ref, first):
    C = RET_CHUNK
    n_chunks = q_ref.shape[0] // C

    @pl.when(first)
    def _():
        r_ref[...] = jnp.zeros_like(r_ref)

    lane = lax.broadcasted_iota(jnp.int32, (C, HEAD_DIM), 1)
    even = (lane % 2) == 0

    def rot(t, cosv, sinv):
        partner = jnp.where(even, pltpu.roll(t, HEAD_DIM - 1, 1), pltpu.roll(t, 1, 1))
        return t * cosv + partner * sinv

    def head(h):
        cols = slice(h * HEAD_DIM, (h + 1) * HEAD_DIM)
        for c in range(n_chunks):
            rows = slice(c * C, (c + 1) * C)
            cosv = cos_ref[rows, :]
            sinv = sin_ref[rows, :]
            q = rot(q_ref[rows, cols].astype(F32), cosv, sinv)
            k = rot(k_ref[rows, cols].astype(F32), cosv, sinv)
            v = v_ref[rows, cols]
            qb = q.astype(BF16)
            yield
            inner = lax.dot_general(qb, k.astype(BF16), NT_DIMS,
                                    preferred_element_type=F32) * dm_ref[h]
            r_old = r_ref[h]
            kz = (k * zeta_ref[h]).astype(BF16)
            r_ref[h] = r_old * cd_ref[h] + lax.dot_general(kz, v, TN_DIMS,
                                                           preferred_element_type=F32)
            yield
            lhs = jnp.concatenate([inner.astype(BF16), (q * xi_ref[h]).astype(BF16)], axis=1)
            rhs = jnp.concatenate([v, r_old.astype(BF16)], axis=0)
            o = jnp.dot(lhs, rhs, preferred_element_type=F32)
            yield
            o = o * lax.rsqrt(jnp.mean(o * o, axis=-1, keepdims=True) + EPS)
            o_ref[rows, cols] = (o * _silu(z_ref[rows, cols].astype(F32))).astype(o_ref.dtype)
            yield

    return [head(h) for h in range(N_HEADS)]


def _attn_kernel(n_cast, rb_ref, qa_ref, ka_ref, va_ref, za_ref, tab_ref,
                 cd_ref, mixb_ref, rope_ref, dm_ref, zeta_ref, xi_ref, hx_ref, wg_ref, *refs):
    w32_refs = refs[:n_cast]
    ya_ref, yb_ref, gate_ref = refs[n_cast:n_cast + 3]
    w16_refs = refs[n_cast + 3:2 * n_cast + 3]
    vt_ref, s_ref, r_ref = refs[2 * n_cast + 3:]

    def gate_proj():
        tm, tn = gate_ref.shape
        for _ in range(GATE_DELAY):
            yield
        for r0 in range(0, tm, GATE_ROWS):
            for c0 in range(0, tn, GATE_COLS):
                acc = jnp.dot(hx_ref[r0:r0 + GATE_ROWS, :],
                              wg_ref[:, c0:c0 + GATE_COLS].astype(BF16),
                              preferred_element_type=F32)
                gate_ref[r0:r0 + GATE_ROWS, c0:c0 + GATE_COLS] = acc.astype(gate_ref.dtype)
                for _ in range(GATE_EVERY):
                    yield

    def casts():
        for w32, w16 in zip(w32_refs, w16_refs):
            w16[...] = w32[...].astype(BF16)
            yield

    qb_ref, kb_ref, vb_ref, zb_ref = (mixb_ref.at[:, c * D_A:(c + 1) * D_A] for c in range(4))
    cos_ref, sin_ref = (rope_ref.at[:, c * HEAD_DIM:(c + 1) * HEAD_DIM] for c in range(2))
    heads = _ret_chains(cd_ref, qb_ref, kb_ref, vb_ref, zb_ref, cos_ref, sin_ref, dm_ref, zeta_ref,
                        xi_ref, yb_ref, r_ref, first=pl.program_id(1) == 0)
    lanes = [_delayed(_chain(heads[i::RET_LANES]), RET_DELAY) for i in range(RET_LANES)]
    moba = _moba_sweeps(rb_ref, qa_ref, ka_ref, va_ref, za_ref, tab_ref, ya_ref, vt_ref, s_ref)
    _round_robin([moba] + lanes + [casts(), gate_proj()])


def _attention(proj3, tabs, rel_bias, weights, hx, w_in, gate_col0, tm, tn):
    B, S, _ = proj3.shape
    L = MOBA_BLOCK
    C = RET_CHUNK
    dk = HEAD_DIM
    rows_b = S // N_HEADS
    assert sorted(qi for grp in MOBA_GROUPS for qi in grp) == list(range(S // L))
    assert rows_b % C == 0
    n_steps = B * N_HEADS
    gate_tiles = 2 * D_MODEL // tn
    assert (B * S // tm) * gate_tiles == n_steps
    rope, decay, zeta, xi, chunk_decay = _retention_tables(S)

    def gate_tile(b, h):
        step = b * N_HEADS + h
        return step // gate_tiles, step % gate_tiles

    def head_cols(col0):
        return pl.BlockSpec((None, S, HEAD_DIM), lambda b, h: (b, 0, col0 + h))

    def all_heads(col0):
        return pl.BlockSpec((None, rows_b, D_A), lambda b, h: (b, h, col0))

    def w_slice(w):
        rows = max(w.shape[0] // n_steps, BF16_SUBLANES)
        n_slices = w.shape[0] // rows
        assert rows * n_slices == w.shape[0] and n_slices <= n_steps, w.shape
        return pl.BlockSpec((rows, w.shape[1]),
                            lambda b, h: (jnp.minimum(b * N_HEADS + h, n_slices - 1), 0))

    head_tab = pl.BlockSpec((N_HEADS, C, dk), lambda b, h: (0, 0, 0))
    w_specs = [w_slice(w) for w in weights]
    smem = pl.BlockSpec(memory_space=pltpu.SMEM)
    outs = pl.pallas_call(
        functools.partial(_attn_kernel, len(weights)),
        name="attention",
        grid=(B, N_HEADS),
        in_specs=[
            smem,
            head_cols(COL_QA), head_cols(COL_KA), head_cols(COL_VA), head_cols(COL_ZA),
            pl.BlockSpec((None, 2, L, L), lambda b, h: (h, 0, 0, 0)),
            smem,
            pl.BlockSpec((None, rows_b, 4 * D_A), lambda b, h: (b, h, COL_MIXB)),
            pl.BlockSpec((rows_b, 2 * dk), lambda b, h: (h, 0)),
            head_tab, head_tab, head_tab,
            pl.BlockSpec((tm, hx.shape[1]), lambda b, h: (gate_tile(b, h)[0], 0)),
            pl.BlockSpec((w_in.shape[0], tn),
                         lambda b, h: (0, gate_col0 // tn + gate_tile(b, h)[1])),
            *w_specs,
        ],
        out_specs=[head_cols(0), all_heads(0), pl.BlockSpec((tm, tn), gate_tile), *w_specs],
        out_shape=[jax.ShapeDtypeStruct((B, S, D_A), BF16)] * 2
        + [jax.ShapeDtypeStruct((B * S, 2 * D_MODEL), BF16)]
        + [jax.ShapeDtypeStruct(w.shape, BF16) for w in weights],
        scratch_shapes=[
            pltpu.VMEM((HEAD_DIM + VT_PAD, S), BF16),
            pltpu.VMEM((2 * len(MOBA_GROUPS[0]), S, L), F32),
            pltpu.VMEM((N_HEADS, dk, dk), F32),
        ],
        compiler_params=pltpu.CompilerParams(
            dimension_semantics=("arbitrary", "arbitrary"),
            vmem_limit_bytes=VMEM_LIMIT_V7X),
    )(rel_bias, proj3, proj3, proj3, proj3, tabs,
      chunk_decay, proj3, rope, decay, zeta, xi, hx, w_in, *weights)
    return outs[0], outs[1], outs[2], outs[3:]


def _tail_kernel(ya_ref, yb_ref, ga_ref, gb_ref, x_ref, p_ref, wa_ref, wb_ref, wo_ref, wpg_ref,
                 wpp_ref, gple_ref, gfin_ref, o_ref, x1_ref, pe_ref):
    s = pl.program_id(0)
    last = pl.num_programs(0) - 1

    def first_half():
        ua = jnp.dot(ya_ref[...], wa_ref[...], preferred_element_type=F32)
        yield
        ub = jnp.dot(yb_ref[...], wb_ref[...], preferred_element_type=F32)
        merged = (jax.nn.sigmoid(ga_ref[...].astype(F32)) * ua
                  + jax.nn.sigmoid(gb_ref[...].astype(F32)) * ub)
        yield
        x1 = x_ref[...] + jnp.dot(merged.astype(BF16), wo_ref[...], preferred_element_type=F32)
        pe = jnp.dot(p_ref[...].astype(BF16), wpp_ref[...], preferred_element_type=F32)
        yield
        x1_ref[...] = x1
        pe_ref[...] = pe

    def second_half(x1, pe):
        hp = _rms_norm(x1, gple_ref[...]).astype(BF16)
        yield
        gate = jax.nn.sigmoid(jnp.dot(hp, wpg_ref[...], preferred_element_type=F32))
        x2 = x1 + gate * pe
        yield
        o_ref[...] = _rms_norm(x2, gfin_ref[...])

    @pl.when(s == 0)
    def _():
        _round_robin([first_half()])

    @pl.when((s > 0) & (s < last))
    def _():
        _round_robin([second_half(x1_ref[...], pe_ref[...]), first_half()])

    @pl.when(s == last)
    def _():
        _round_robin([second_half(x1_ref[...], pe_ref[...])])


def _tail(ya, yb, gates, x2d, p2d, wa, wb, wo, wpg, wpp, g_ple, g_final, tm):
    m, d = x2d.shape
    n_t = m // tm

    def const(shape):
        return pl.BlockSpec(shape, lambda s: (0, 0), pipeline_mode=pl.Buffered(1))

    def first(s):
        return jnp.minimum(s, n_t - 1)

    def second(s):
        return jnp.maximum(s - 1, 0)

    return pl.pallas_call(
        _tail_kernel,
        name="tail",
        grid=(n_t + 1,),
        in_specs=[
            pl.BlockSpec((tm, D_A), lambda s: (first(s), 0)),
            pl.BlockSpec((tm, D_A), lambda s: (first(s), 0)),
            pl.BlockSpec((tm, d), lambda s: (first(s), 0)),
            pl.BlockSpec((tm, d), lambda s: (first(s), 1)),
            pl.BlockSpec((tm, d), lambda s: (first(s), 0)),
            pl.BlockSpec((tm, D_PLE), lambda s: (first(s), 0)),
            const((D_A, d)), const((D_A, d)), const((d, d)), const((d, d)), const((D_PLE, d)),
            const((1, d)), const((1, d)),
        ],
        out_specs=pl.BlockSpec((tm, d), lambda s: (second(s), 0)),
        out_shape=jax.ShapeDtypeStruct((m, d), F32),
        scratch_shapes=[pltpu.VMEM((tm, d), F32), pltpu.VMEM((tm, d), F32)],
        compiler_params=pltpu.CompilerParams(
            dimension_semantics=("arbitrary",),
            vmem_limit_bytes=VMEM_LIMIT_V7X),
    )(ya, yb, gates, gates, x2d, p2d, wa, wb, wo, wpg, wpp, g_ple, g_final)


def kernel(x, p, g_mix, w_in, w_a, w_b, w_out, g_ple, w_ple_gate, w_ple_proj, rel_bias, g_final):
    B, S, d = x.shape
    assert w_in.shape[0] == 1, "the fused tail applies the final norm: single-layer stacks only"
    tabs = _bias_tables(rel_bias)
    x2d = x.reshape(B * S, d)
    n_mix = 8 * D_A
    col_scale = jnp.asarray(
        np.where(np.arange(n_mix) < D_A, HEAD_DIM ** -0.5 * LOG2E, 1.0).astype(np.float32))
    proj, hx = _project(x2d, g_mix, w_in[0], col_scale[None, :], n_mix, PROJ_TM, PROJ_TN)
    proj3 = proj.reshape(B, S, n_mix)
    ya, yb, gates, (wa, wb, wo, wpg, wpp) = _attention(
        proj3, tabs, rel_bias, (w_a[0], w_b[0], w_out[0], w_ple_gate[0], w_ple_proj[0]),
        hx, w_in[0], n_mix, PROJ_TM, PROJ_TN)
    out = _tail(ya.reshape(B * S, D_A), yb.reshape(B * S, D_A), gates, x2d,
                p[0].reshape(B * S, D_PLE), wa, wb, wo, wpg, wpp,
                g_ple, g_final[None, :], TAIL_TM)
    return out.reshape(B, S, d)
```

```python
import functools
import math

import numpy as np
import jax
import jax.numpy as jnp
from jax import lax
from jax.experimental import pallas as pl
from jax.experimental.pallas import tpu as pltpu

D_MODEL = 2048
N_HEADS = 8
HEAD_DIM = 128
D_A = N_HEADS * HEAD_DIM
MOBA_BLOCK = 256
MOBA_TOPK = 3
N_BUCKETS = 32
MAX_DISTANCE = 128
RET_CHUNK = 128
ROT_BASE = 10000.0
D_PLE = 256
EPS = 1e-6

COL_QA, COL_KA, COL_VA, COL_ZA = (i * N_HEADS for i in range(4))
COL_MIXB = 1

VMEM_LIMIT_V7X = 56 * 1024 * 1024
PROJ_TM = 1024
PROJ_TN = 1024
TAIL_TM = 256
BF16_SUBLANES = 16
VT_PAD = BF16_SUBLANES
MOBA_GROUPS = ((7, 0), (6, 1), (5, 2), (4, 3))
RET_LANES = 2
RET_DELAY = 5
X_PARTS = 4
GATE_ROWS = 1024
GATE_COLS = 256
GATE_EVERY = 8
GATE_DELAY = 2

F32 = jnp.float32
BF16 = jnp.bfloat16
NEG_INF = float("-inf")
LOG2E = math.log2(math.e)
NT_DIMS = (((1,), (1,)), ((), ()))
TN_DIMS = (((0,), (0,)), ((), ()))


def _rms_norm(x, g):
    return x * lax.rsqrt(jnp.mean(x * x, axis=-1, keepdims=True) + EPS) * g


def _silu(z):
    return z * jax.nn.sigmoid(z)


def _sweeps(gens):
    gens = list(gens)
    while gens:
        for g in list(gens):
            try:
                next(g)
            except StopIteration:
                gens.remove(g)
        yield


def _round_robin(gens):
    for _ in _sweeps(gens):
        pass


def _chain(gens):
    for g in gens:
        yield from g


def _delayed(gen, sweeps):
    for _ in range(sweeps):
        yield
    yield from gen


def _proj_kernel(*refs):
    x_refs, (g_ref, w_ref, cs_ref, o_ref, h_ref) = refs[:X_PARTS], refs[X_PARTS:]

    rows = h_ref.shape[0] // X_PARTS

    @pl.when(pl.program_id(1) == 0)
    def _():
        w = w_ref[...].astype(BF16)
        for q, x_ref in enumerate(x_refs):
            part = slice(q * rows, (q + 1) * rows)
            hq = _rms_norm(x_ref[...], g_ref[...]).astype(BF16)
            h_ref[part, :] = hq
            acc = jnp.dot(hq, w, preferred_element_type=F32)
            o_ref[part, :] = (acc * cs_ref[...]).astype(o_ref.dtype)

    @pl.when(pl.program_id(1) != 0)
    def _():
        acc = jnp.dot(h_ref[...], w_ref[...].astype(BF16), preferred_element_type=F32)
        o_ref[...] = (acc * cs_ref[...]).astype(o_ref.dtype)


def _project(x2d, g, w, col_scale, n, tm, tn):
    m, d = x2d.shape
    n_i, n_j = m // tm, n // tn
    assert n_j > X_PARTS, "every x slice must still hold this row tile at column step 0"

    def x_part(q):
        def index(i, j):
            nxt = jnp.minimum(i + (j >= n_j - X_PARTS + q).astype(jnp.int32), n_i - 1)
            return (nxt * X_PARTS + q, 0)
        return pl.BlockSpec((tm // X_PARTS, d), index)

    return pl.pallas_call(
        _proj_kernel,
        name="proj",
        grid=(n_i, n_j),
        in_specs=[
            *[x_part(q) for q in range(X_PARTS)],
            pl.BlockSpec((1, d), lambda i, j: (0, 0)),
            pl.BlockSpec((d, tn), lambda i, j: (0, j)),
            pl.BlockSpec((1, tn), lambda i, j: (0, j)),
        ],
        out_specs=[pl.BlockSpec((tm, tn), lambda i, j: (i, j)),
                   pl.BlockSpec((tm, d), lambda i, j: (i, 0))],
        out_shape=[jax.ShapeDtypeStruct((m, n), BF16), jax.ShapeDtypeStruct((m, d), BF16)],
        compiler_params=pltpu.CompilerParams(
            dimension_semantics=("arbitrary", "arbitrary"),
            vmem_limit_bytes=VMEM_LIMIT_V7X),
    )(*[x2d] * X_PARTS, g, w, col_scale)


def _t5_bucket_np(n):
    max_exact = N_BUCKETS // 2
    nf = np.maximum(n, 1).astype(np.float32)
    val = (np.log(nf / max_exact) / np.float32(math.log(MAX_DISTANCE / max_exact))
           * (N_BUCKETS - max_exact))
    large = np.minimum(max_exact + val.astype(np.int32), N_BUCKETS - 1)
    return np.where(n < max_exact, n, large).astype(np.int32)


def _bucket_row():
    return _t5_bucket_np(np.arange(2 * MOBA_BLOCK))[None, :].astype(np.int32)


def _bias_kernel(rb_ref, bkt_ref, o_ref):
    L = MOBA_BLOCK
    bkt = bkt_ref[...]
    key = lax.broadcasted_iota(jnp.int32, (L, L), 0)
    qry = lax.broadcasted_iota(jnp.int32, (L, L), 1)
    for h in range(N_HEADS):
        f = jnp.zeros(bkt.shape, F32)
        for b in range(N_BUCKETS):
            f = jnp.where(bkt == b, rb_ref[b, h] * LOG2E, f)
        both = pltpu.roll(jnp.broadcast_to(f, (L, 2 * L)), 0, 1, stride=1, stride_axis=0)
        o_ref[h, 0] = jnp.where(qry >= key, both[:, :L], NEG_INF)
        o_ref[h, 1] = both[:, L:]


def _bias_tables(rel_bias):
    bkt = jnp.asarray(_bucket_row())
    L = MOBA_BLOCK
    return pl.pallas_call(
        _bias_kernel,
        name="t5_bias",
        in_specs=[
            pl.BlockSpec(memory_space=pltpu.SMEM),
            pl.BlockSpec(memory_space=pltpu.VMEM),
        ],
        out_specs=pl.BlockSpec(memory_space=pltpu.VMEM),
        out_shape=jax.ShapeDtypeStruct((N_HEADS, 2, L, L), F32),
    )(rel_bias, bkt)


def _moba_sweeps(rb_ref, q_ref, k_ref, v_ref, z_ref, tab_ref, o_ref, vt_ref, s_ref):
    L = MOBA_BLOCK
    nb = k_ref.shape[0] // L
    h = pl.program_id(1)

    kf = k_ref[...].astype(F32).reshape(nb, L, HEAD_DIM)
    km = jnp.mean(kf, axis=1)
    km_hi = km.astype(BF16)
    km_lo = (km - km_hi.astype(F32)).astype(BF16)
    km2 = jnp.concatenate([km_hi, km_lo], axis=0)
    vt_ref[:HEAD_DIM, :] = v_ref[...].T
    ones_row = lax.broadcasted_iota(jnp.int32, (VT_PAD, vt_ref.shape[1]), 0) == 0
    vt_ref[HEAD_DIM:, :] = jnp.where(ones_row, 1.0, 0.0).astype(BF16)
    far_bias = rb_ref[N_BUCKETS - 1, h] * LOG2E
    row = lax.broadcasted_iota(jnp.int32, (nb, L), 0)

    def scores(qi, slot, out):
        q = q_ref[qi * L:(qi + 1) * L, :]
        sel_add = None
        if qi > MOBA_TOPK:
            g2 = lax.dot_general(km2, q, NT_DIMS, preferred_element_type=F32)
            g = g2[:nb] + g2[nb:]
            cnt = jnp.zeros((nb, L), jnp.int32)
            for mm in range(qi):
                gm = g[mm:mm + 1, :]
                ge = jnp.where(gm >= g, 1, 0)
                gt = jnp.where(gm > g, 1, 0)
                cnt = cnt + jnp.where(row > mm, ge, gt)
            sel_add = jnp.where(cnt < MOBA_TOPK, 0.0, NEG_INF)

        sbuf = s_ref.at[slot]
        shifts = []
        m = None
        for n in range(qi + 1):
            s = lax.dot_general(k_ref[n * L:(n + 1) * L, :], q, NT_DIMS,
                                preferred_element_type=F32)
            shift = None
            if n == qi:
                s = s + tab_ref[0]
            else:
                if n == qi - 1:
                    s = s + tab_ref[1]
                if sel_add is not None:
                    shift = sel_add[n:n + 1, :]
                if n < qi - 1:
                    shift = far_bias if shift is None else shift + far_bias
            sbuf[n * L:(n + 1) * L, :] = s
            tmax = jnp.max(s, axis=0, keepdims=True)
            if shift is not None:
                tmax = tmax + shift
            shifts.append(shift)
            m = tmax if m is None else jnp.maximum(m, tmax)
            yield
        out.append((m, shifts))

    def outputs(qi, slot, m, shifts):
        sbuf = s_ref.at[slot]
        acc = None
        for n in range(qi + 1):
            off = m if shifts[n] is None else m - shifts[n]
            p = jnp.exp2((sbuf[n * L:(n + 1) * L, :] - off).astype(BF16))
            pv = jnp.dot(vt_ref[:, n * L:(n + 1) * L], p,
                         preferred_element_type=F32)
            acc = pv if acc is None else acc + pv
            yield
        l = acc[HEAD_DIM:HEAD_DIM + 1, :]
        y = (acc[:HEAD_DIM, :] * (1.0 / l)).T
        z = z_ref[qi * L:(qi + 1) * L, :].astype(F32)
        o_ref[qi * L:(qi + 1) * L, :] = (y * _silu(z)).astype(o_ref.dtype)

    width = len(MOBA_GROUPS[0])
    pending = []
    for t, grp in enumerate(MOBA_GROUPS):
        outs = [[] for _ in grp]
        slots = [(t % 2) * width + i for i in range(width)]
        yield from _sweeps([scores(qi, slots[i], outs[i]) for i, qi in enumerate(grp)]
                           + [outputs(*args) for args in pending])
        pending = [(qi, slots[i]) + outs[i][0] for i, qi in enumerate(grp)]
    yield from _sweeps([outputs(*args) for args in pending])


def _retention_tables(S):
    C = RET_CHUNK
    dk = HEAD_DIM
    f32 = np.float32
    pos = np.arange(S, dtype=f32)
    theta = (1.0 / (f32(ROT_BASE) ** np.linspace(0.0, 1.0, dk // 2, dtype=f32))).astype(f32)
    ang = pos[:, None] * theta[None, :]
    cos = np.repeat(np.cos(ang), 2, axis=1)
    sin = np.repeat(np.sin(ang), 2, axis=1) * np.tile(np.array([-1.0, 1.0], f32), dk // 2)
    log_gamma = np.log(1.0 - 2.0 ** (-5.0 - np.arange(N_HEADS, dtype=f32))).astype(f32)
    i = np.arange(C)
    diff = i[:, None] - i[None, :]
    k_scale = f32(dk ** -0.5)
    decay = np.where(diff >= 0,
                     np.exp(np.maximum(diff, 0).astype(f32) * log_gamma[:, None, None]),
                     0.0) * k_scale
    zeta = np.exp((C - 1 - i).astype(f32)[None, :] * log_gamma[:, None]) * k_scale
    xi = np.exp((i + 1).astype(f32)[None, :] * log_gamma[:, None])
    zeta = np.broadcast_to(zeta[:, :, None], (N_HEADS, C, dk))
    xi = np.broadcast_to(xi[:, :, None], (N_HEADS, C, dk))
    chunk_decay = np.exp(C * log_gamma)
    rope = np.concatenate([cos, sin], axis=1)
    return tuple(jnp.asarray(np.ascontiguousarray(t, dtype=f32))
                 for t in (rope, decay, zeta, xi, chunk_decay))


def _ret_chains(cd_ref, q_ref, k_ref, v_ref, z_ref, cos_ref, sin_ref, dm_ref, zeta_ref, xi_ref,
                o_ref, r_ref, first):
    C = RET_CHUNK
    n_chunks = q_ref.shape[0] // C

    @pl.when(first)
    def _():
        r_ref[...] = jnp.zeros_like(r_ref)

    lane = lax.broadcasted_iota(jnp.int32, (C, HEAD_DIM), 1)
    even = (lane % 2) == 0

    def rot(t, cosv, sinv):
        partner = jnp.where(even, pltpu.roll(t, HEAD_DIM - 1, 1), pltpu.roll(t, 1, 1))
        return t * cosv + partner * sinv

    def head(h):
        cols = slice(h * HEAD_DIM, (h + 1) * HEAD_DIM)
        for c in range(n_chunks):
            rows = slice(c * C, (c + 1) * C)
            cosv = cos_ref[rows, :]
            sinv = sin_ref[rows, :]
            q = rot(q_ref[rows, cols].astype(F32), cosv, sinv)
            k = rot(k_ref[rows, cols].astype(F32), cosv, sinv)
            v = v_ref[rows, cols]
            qb = q.astype(BF16)
            yield
            inner = lax.dot_general(qb, k.astype(BF16), NT_DIMS,
                                    preferred_element_type=F32) * dm_ref[h]
            r_old = r_ref[h]
            kz = (k * zeta_ref[h]).astype(BF16)
            r_ref[h] = r_old * cd_ref[h] + lax.dot_general(kz, v, TN_DIMS,
                                                           preferred_element_type=F32)
            yield
            lhs = jnp.concatenate([inner.astype(BF16), (q * xi_ref[h]).astype(BF16)], axis=1)
            rhs = jnp.concatenate([v, r_old.astype(BF16)], axis=0)
            o = jnp.dot(lhs, rhs, preferred_element_type=F32)
            yield
            o = o * lax.rsqrt(jnp.mean(o * o, axis=-1, keepdims=True) + EPS)
            o_ref[rows, cols] = (o * _silu(z_ref[rows, cols].astype(F32))).astype(o_ref.dtype)
            yield

    return [head(h) for h in range(N_HEADS)]


def _attn_kernel(n_cast, rb_ref, qa_ref, ka_ref, va_ref, za_ref, tab_ref,
                 cd_ref, mixb_ref, rope_ref, dm_ref, zeta_ref, xi_ref, hx_ref, wg_ref, *refs):
    w32_refs = refs[:n_cast]
    ya_ref, yb_ref, gate_ref = refs[n_cast:n_cast + 3]
    w16_refs = refs[n_cast + 3:2 * n_cast + 3]
    vt_ref, s_ref, r_ref = refs[2 * n_cast + 3:]

    def gate_proj():
        tm, tn = gate_ref.shape
        for _ in range(GATE_DELAY):
            yield
        for r0 in range(0, tm, GATE_ROWS):
            for c0 in range(0, tn, GATE_COLS):
                acc = jnp.dot(hx_ref[r0:r0 + GATE_ROWS, :],
                              wg_ref[:, c0:c0 + GATE_COLS].astype(BF16),
                              preferred_element_type=F32)
                gate_ref[r0:r0 + GATE_ROWS, c0:c0 + GATE_COLS] = acc.astype(gate_ref.dtype)
                for _ in range(GATE_EVERY):
                    yield

    def casts():
        for w32, w16 in zip(w32_refs, w16_refs):
            w16[...] = w32[...].astype(BF16)
            yield

    qb_ref, kb_ref, vb_ref, zb_ref = (mixb_ref.at[:, c * D_A:(c + 1) * D_A] for c in range(4))
    cos_ref, sin_ref = (rope_ref.at[:, c * HEAD_DIM:(c + 1) * HEAD_DIM] for c in range(2))
    heads = _ret_chains(cd_ref, qb_ref, kb_ref, vb_ref, zb_ref, cos_ref, sin_ref, dm_ref, zeta_ref,
                        xi_ref, yb_ref, r_ref, first=pl.program_id(1) == 0)
    lanes = [_delayed(_chain(heads[i::RET_LANES]), RET_DELAY) for i in range(RET_LANES)]
    moba = _moba_sweeps(rb_ref, qa_ref, ka_ref, va_ref, za_ref, tab_ref, ya_ref, vt_ref, s_ref)
    _round_robin([moba] + lanes + [casts(), gate_proj()])


def _attention(proj3, tabs, rel_bias, weights, hx, w_in, gate_col0, tm, tn):
    B, S, _ = proj3.shape
    L = MOBA_BLOCK
    C = RET_CHUNK
    dk = HEAD_DIM
    rows_b = S // N_HEADS
    assert sorted(qi for grp in MOBA_GROUPS for qi in grp) == list(range(S // L))
    assert rows_b % C == 0
    n_steps = B * N_HEADS
    gate_tiles = 2 * D_MODEL // tn
    assert (B * S // tm) * gate_tiles == n_steps
    rope, decay, zeta, xi, chunk_decay = _retention_tables(S)

    def gate_tile(b, h):
        step = b * N_HEADS + h
        return step // gate_tiles, step % gate_tiles

    def head_cols(col0):
        return pl.BlockSpec((None, S, HEAD_DIM), lambda b, h: (b, 0, col0 + h))

    def all_heads(col0):
        return pl.BlockSpec((None, rows_b, D_A), lambda b, h: (b, h, col0))

    def w_slice(w):
        rows = max(w.shape[0] // n_steps, BF16_SUBLANES)
        n_slices = w.shape[0] // rows
        assert rows * n_slices == w.shape[0] and n_slices <= n_steps, w.shape
        return pl.BlockSpec((rows, w.shape[1]),
                            lambda b, h: (jnp.minimum(b * N_HEADS + h, n_slices - 1), 0))

    head_tab = pl.BlockSpec((N_HEADS, C, dk), lambda b, h: (0, 0, 0))
    w_specs = [w_slice(w) for w in weights]
    smem = pl.BlockSpec(memory_space=pltpu.SMEM)
    outs = pl.pallas_call(
        functools.partial(_attn_kernel, len(weights)),
        name="attention",
        grid=(B, N_HEADS),
        in_specs=[
            smem,
            head_cols(COL_QA), head_cols(COL_KA), head_cols(COL_VA), head_cols(COL_ZA),
            pl.BlockSpec((None, 2, L, L), lambda b, h: (h, 0, 0, 0)),
            smem,
            pl.BlockSpec((None, rows_b, 4 * D_A), lambda b, h: (b, h, COL_MIXB)),
            pl.BlockSpec((rows_b, 2 * dk), lambda b, h: (h, 0)),
            head_tab, head_tab, head_tab,
            pl.BlockSpec((tm, hx.shape[1]), lambda b, h: (gate_tile(b, h)[0], 0)),
            pl.BlockSpec((w_in.shape[0], tn),
                         lambda b, h: (0, gate_col0 // tn + gate_tile(b, h)[1])),
            *w_specs,
        ],
        out_specs=[head_cols(0), all_heads(0), pl.BlockSpec((tm, tn), gate_tile), *w_specs],
        out_shape=[jax.ShapeDtypeStruct((B, S, D_A), BF16)] * 2
        + [jax.ShapeDtypeStruct((B * S, 2 * D_MODEL), BF16)]
        + [jax.ShapeDtypeStruct(w.shape, BF16) for w in weights],
        scratch_shapes=[
            pltpu.VMEM((HEAD_DIM + VT_PAD, S), BF16),
            pltpu.VMEM((2 * len(MOBA_GROUPS[0]), S, L), F32),
            pltpu.VMEM((N_HEADS, dk, dk), F32),
        ],
        compiler_params=pltpu.CompilerParams(
            dimension_semantics=("arbitrary", "arbitrary"),
            vmem_limit_bytes=VMEM_LIMIT_V7X),
    )(rel_bias, proj3, proj3, proj3, proj3, tabs,
      chunk_decay, proj3, rope, decay, zeta, xi, hx, w_in, *weights)
    return outs[0], outs[1], outs[2], outs[3:]


def _tail_kernel(ya_ref, yb_ref, ga_ref, gb_ref, x_ref, p_ref, wa_ref, wb_ref, wo_ref, wpg_ref,
                 wpp_ref, gple_ref, gfin_ref, o_ref, x1_ref, pe_ref):
    s = pl.program_id(0)
    last = pl.num_programs(0) - 1

    def first_half():
        ua = jnp.dot(ya_ref[...], wa_ref[...], preferred_element_type=F32)
        yield
        ub = jnp.dot(yb_ref[...], wb_ref[...], preferred_element_type=F32)
        merged = (jax.nn.sigmoid(ga_ref[...].astype(F32)) * ua
                  + jax.nn.sigmoid(gb_ref[...].astype(F32)) * ub)
        yield
        x1 = x_ref[...] + jnp.dot(merged.astype(BF16), wo_ref[...], preferred_element_type=F32)
        pe = jnp.dot(p_ref[...].astype(BF16), wpp_ref[...], preferred_element_type=F32)
        yield
        x1_ref[...] = x1
        pe_ref[...] = pe

    def second_half(x1, pe):
        hp = _rms_norm(x1, gple_ref[...]).astype(BF16)
        yield
        gate = jax.nn.sigmoid(jnp.dot(hp, wpg_ref[...], preferred_element_type=F32))
        x2 = x1 + gate * pe
        yield
        o_ref[...] = _rms_norm(x2, gfin_ref[...])

    @pl.when(s == 0)
    def _():
        _round_robin([first_half()])

    @pl.when((s > 0) & (s < last))
    def _():
        _round_robin([second_half(x1_ref[...], pe_ref[...]), first_half()])

    @pl.when(s == last)
    def _():
        _round_robin([second_half(x1_ref[...], pe_ref[...])])


def _tail(ya, yb, gates, x2d, p2d, wa, wb, wo, wpg, wpp, g_ple, g_final, tm):
    m, d = x2d.shape
    n_t = m // tm

    def const(shape):
        return pl.BlockSpec(shape, lambda s: (0, 0), pipeline_mode=pl.Buffered(1))

    def first(s):
        return jnp.minimum(s, n_t - 1)

    def second(s):
        return jnp.maximum(s - 1, 0)

    return pl.pallas_call(
        _tail_kernel,
        name="tail",
        grid=(n_t + 1,),
        in_specs=[
            pl.BlockSpec((tm, D_A), lambda s: (first(s), 0)),
            pl.BlockSpec((tm, D_A), lambda s: (first(s), 0)),
            pl.BlockSpec((tm, d), lambda s: (first(s), 0)),
            pl.BlockSpec((tm, d), lambda s: (first(s), 1)),
            pl.BlockSpec((tm, d), lambda s: (first(s), 0)),
            pl.BlockSpec((tm, D_PLE), lambda s: (first(s), 0)),
            const((D_A, d)), const((D_A, d)), const((d, d)), const((d, d)), const((D_PLE, d)),
            const((1, d)), const((1, d)),
        ],
        out_specs=pl.BlockSpec((tm, d), lambda s: (second(s), 0)),
        out_shape=jax.ShapeDtypeStruct((m, d), F32),
        scratch_shapes=[pltpu.VMEM((tm, d), F32), pltpu.VMEM((tm, d), F32)],
        compiler_params=pltpu.CompilerParams(
            dimension_semantics=("arbitrary",),
            vmem_limit_bytes=VMEM_LIMIT_V7X),
    )(ya, yb, gates, gates, x2d, p2d, wa, wb, wo, wpg, wpp, g_ple, g_final)


def kernel(x, p, g_mix, w_in, w_a, w_b, w_out, g_ple, w_ple_gate, w_ple_proj, rel_bias, g_final):
    B, S, d = x.shape
    assert w_in.shape[0] == 1, "the fused tail applies the final norm: single-layer stacks only"
    tabs = _bias_tables(rel_bias)
    x2d = x.reshape(B * S, d)
    n_mix = 8 * D_A
    col_scale = jnp.asarray(
        np.where(np.arange(n_mix) < D_A, HEAD_DIM ** -0.5 * LOG2E, 1.0).astype(np.float32))
    proj, hx = _project(x2d, g_mix, w_in[0], col_scale[None, :], n_mix, PROJ_TM, PROJ_TN)
    proj3 = proj.reshape(B, S, n_mix)
    ya, yb, gates, (wa, wb, wo, wpg, wpp) = _attention(
        proj3, tabs, rel_bias, (w_a[0], w_b[0], w_out[0], w_ple_gate[0], w_ple_proj[0]),
        hx, w_in[0], n_mix, PROJ_TM, PROJ_TN)
    out = _tail(ya.reshape(B * S, D_A), yb.reshape(B * S, D_A), gates, x2d,
                p[0].reshape(B * S, D_PLE), wa, wb, wo, wpg, wpp,
                g_ple, g_final[None, :], TAIL_TM)
    return out.reshape(B, S, d)
```

```python
import functools
import math

import numpy as np
import jax
import jax.numpy as jnp
from jax import lax
from jax.experimental import pallas as pl
from jax.experimental.pallas import tpu as pltpu

D_MODEL = 2048
N_HEADS = 8
HEAD_DIM = 128
D_A = N_HEADS * HEAD_DIM
MOBA_BLOCK = 256
MOBA_TOPK = 3
N_BUCKETS = 32
MAX_DISTANCE = 128
RET_CHUNK = 128
ROT_BASE = 10000.0
D_PLE = 256
EPS = 1e-6

COL_QA, COL_KA, COL_VA, COL_ZA = (i * N_HEADS for i in range(4))
COL_MIXB = 1

VMEM_LIMIT_V7X = 56 * 1024 * 1024
PROJ_TM = 1024
PROJ_TN = 1024
TAIL_TM = 256
BF16_SUBLANES = 16
VT_PAD = BF16_SUBLANES
MOBA_GROUPS = ((7, 0), (6, 1), (5, 2), (4, 3))
RET_LANES = 2
RET_DELAY = 5
X_PARTS = 4
GATE_ROWS = 1024
GATE_COLS = 256
GATE_EVERY = 8
GATE_DELAY = 2

F32 = jnp.float32
BF16 = jnp.bfloat16
NEG_INF = float("-inf")
LOG2E = math.log2(math.e)
NT_DIMS = (((1,), (1,)), ((), ()))
TN_DIMS = (((0,), (0,)), ((), ()))


def _rms_norm(x, g):
    return x * lax.rsqrt(jnp.mean(x * x, axis=-1, keepdims=True) + EPS) * g


def _silu(z):
    return z * jax.nn.sigmoid(z)


def _sweeps(gens):
    gens = list(gens)
    while gens:
        for g in list(gens):
            try:
                next(g)
            except StopIteration:
                gens.remove(g)
        yield


def _round_robin(gens):
    for _ in _sweeps(gens):
        pass


def _chain(gens):
    for g in gens:
        yield from g


def _delayed(gen, sweeps):
    for _ in range(sweeps):
        yield
    yield from gen


def _proj_kernel(*refs):
    x_refs, (g_ref, w_ref, cs_ref, o_ref, h_ref) = refs[:X_PARTS], refs[X_PARTS:]

    rows = h_ref.shape[0] // X_PARTS

    @pl.when(pl.program_id(1) == 0)
    def _():
        w = w_ref[...].astype(BF16)
        for q, x_ref in enumerate(x_refs):
            part = slice(q * rows, (q + 1) * rows)
            hq = _rms_norm(x_ref[...], g_ref[...]).astype(BF16)
            h_ref[part, :] = hq
            acc = jnp.dot(hq, w, preferred_element_type=F32)
            o_ref[part, :] = (acc * cs_ref[...]).astype(o_ref.dtype)

    @pl.when(pl.program_id(1) != 0)
    def _():
        acc = jnp.dot(h_ref[...], w_ref[...].astype(BF16), preferred_element_type=F32)
        o_ref[...] = (acc * cs_ref[...]).astype(o_ref.dtype)


def _project(x2d, g, w, col_scale, n, tm, tn):
    m, d = x2d.shape
    n_i, n_j = m // tm, n // tn
    assert n_j > X_PARTS, "every x slice must still hold this row tile at column step 0"

    def x_part(q):
        def index(i, j):
            nxt = jnp.minimum(i + (j >= n_j - X_PARTS + q).astype(jnp.int32), n_i - 1)
            return (nxt * X_PARTS + q, 0)
        return pl.BlockSpec((tm // X_PARTS, d), index)

    return pl.pallas_call(
        _proj_kernel,
        name="proj",
        grid=(n_i, n_j),
        in_specs=[
            *[x_part(q) for q in range(X_PARTS)],
            pl.BlockSpec((1, d), lambda i, j: (0, 0)),
            pl.BlockSpec((d, tn), lambda i, j: (0, j)),
            pl.BlockSpec((1, tn), lambda i, j: (0, j)),
        ],
        out_specs=[pl.BlockSpec((tm, tn), lambda i, j: (i, j)),
                   pl.BlockSpec((tm, d), lambda i, j: (i, 0))],
        out_shape=[jax.ShapeDtypeStruct((m, n), BF16), jax.ShapeDtypeStruct((m, d), BF16)],
        compiler_params=pltpu.CompilerParams(
            dimension_semantics=("arbitrary", "arbitrary"),
            vmem_limit_bytes=VMEM_LIMIT_V7X),
    )(*[x2d] * X_PARTS, g, w, col_scale)


def _t5_bucket_np(n):
    max_exact = N_BUCKETS // 2
    nf = np.maximum(n, 1).astype(np.float32)
    val = (np.log(nf / max_exact) / np.float32(math.log(MAX_DISTANCE / max_exact))
           * (N_BUCKETS - max_exact))
    large = np.minimum(max_exact + val.astype(np.int32), N_BUCKETS - 1)
    return np.where(n < max_exact, n, large).astype(np.int32)


def _bucket_row():
    return _t5_bucket_np(np.arange(2 * MOBA_BLOCK))[None, :].astype(np.int32)


def _bias_kernel(rb_ref, bkt_ref, o_ref):
    L = MOBA_BLOCK
    bkt = bkt_ref[...]
    key = lax.broadcasted_iota(jnp.int32, (L, L), 0)
    qry = lax.broadcasted_iota(jnp.int32, (L, L), 1)
    for h in range(N_HEADS):
        f = jnp.zeros(bkt.shape, F32)
        for b in range(N_BUCKETS):
            f = jnp.where(bkt == b, rb_ref[b, h] * LOG2E, f)
        both = pltpu.roll(jnp.broadcast_to(f, (L, 2 * L)), 0, 1, stride=1, stride_axis=0)
        o_ref[h, 0] = jnp.where(qry >= key, both[:, :L], NEG_INF)
        o_ref[h, 1] = both[:, L:]


def _bias_tables(rel_bias):
    bkt = jnp.asarray(_bucket_row())
    L = MOBA_BLOCK
    return pl.pallas_call(
        _bias_kernel,
        name="t5_bias",
        in_specs=[
            pl.BlockSpec(memory_space=pltpu.SMEM),
            pl.BlockSpec(memory_space=pltpu.VMEM),
        ],
        out_specs=pl.BlockSpec(memory_space=pltpu.VMEM),
        out_shape=jax.ShapeDtypeStruct((N_HEADS, 2, L, L), F32),
    )(rel_bias, bkt)


def _moba_sweeps(rb_ref, q_ref, k_ref, v_ref, z_ref, tab_ref, o_ref, vt_ref, s_ref):
    L = MOBA_BLOCK
    nb = k_ref.shape[0] // L
    h = pl.program_id(1)

    kf = k_ref[...].astype(F32).reshape(nb, L, HEAD_DIM)
    km = jnp.mean(kf, axis=1)
    km_hi = km.astype(BF16)
    km_lo = (km - km_hi.astype(F32)).astype(BF16)
    km2 = jnp.concatenate([km_hi, km_lo], axis=0)
    vt_ref[:HEAD_DIM, :] = v_ref[...].T
    ones_row = lax.broadcasted_iota(jnp.int32, (VT_PAD, vt_ref.shape[1]), 0) == 0
    vt_ref[HEAD_DIM:, :] = jnp.where(ones_row, 1.0, 0.0).astype(BF16)
    far_bias = rb_ref[N_BUCKETS - 1, h] * LOG2E
    row = lax.broadcasted_iota(jnp.int32, (nb, L), 0)

    def scores(qi, slot, out):
        q = q_ref[qi * L:(qi + 1) * L, :]
        sel_add = None
        if qi > MOBA_TOPK:
            g2 = lax.dot_general(km2, q, NT_DIMS, preferred_element_type=F32)
            g = g2[:nb] + g2[nb:]
            cnt = jnp.zeros((nb, L), jnp.int32)
            for mm in range(qi):
                gm = g[mm:mm + 1, :]
                ge = jnp.where(gm >= g, 1, 0)
                gt = jnp.where(gm > g, 1, 0)
                cnt = cnt + jnp.where(row > mm, ge, gt)
            sel_add = jnp.where(cnt < MOBA_TOPK, 0.0, NEG_INF)

        sbuf = s_ref.at[slot]
        shifts = []
        m = None
        for n in range(qi + 1):
            s = lax.dot_general(k_ref[n * L:(n + 1) * L, :], q, NT_DIMS,
                                preferred_element_type=F32)
            shift = None
            if n == qi:
                s = s + tab_ref[0]
            else:
                if n == qi - 1:
                    s = s + tab_ref[1]
                if sel_add is not None:
                    shift = sel_add[n:n + 1, :]
                if n < qi - 1:
                    shift = far_bias if shift is None else shift + far_bias
            sbuf[n * L:(n + 1) * L, :] = s
            tmax = jnp.max(s, axis=0, keepdims=True)
            if shift is not None:
                tmax = tmax + shift
            shifts.append(shift)
            m = tmax if m is None else jnp.maximum(m, tmax)
            yield
        out.append((m, shifts))

    def outputs(qi, slot, m, shifts):
        sbuf = s_ref.at[slot]
        acc = None
        for n in range(qi + 1):
            off = m if shifts[n] is None else m - shifts[n]
            p = jnp.exp2((sbuf[n * L:(n + 1) * L, :] - off).astype(BF16))
            pv = jnp.dot(vt_ref[:, n * L:(n + 1) * L], p,
                         preferred_element_type=F32)
            acc = pv if acc is None else acc + pv
            yield
        l = acc[HEAD_DIM:HEAD_DIM + 1, :]
        y = (acc[:HEAD_DIM, :] * (1.0 / l)).T
        z = z_ref[qi * L:(qi + 1) * L, :].astype(F32)
        o_ref[qi * L:(qi + 1) * L, :] = (y * _silu(z)).astype(o_ref.dtype)

    width = len(MOBA_GROUPS[0])
    pending = []
    for t, grp in enumerate(MOBA_GROUPS):
        outs = [[] for _ in grp]
        slots = [(t % 2) * width + i for i in range(width)]
        yield from _sweeps([scores(qi, slots[i], outs[i]) for i, qi in enumerate(grp)]
                           + [outputs(*args) for args in pending])
        pending = [(qi, slots[i]) + outs[i][0] for i, qi in enumerate(grp)]
    yield from _sweeps([outputs(*args) for args in pending])


def _retention_tables(S):
    C = RET_CHUNK
    dk = HEAD_DIM
    f32 = np.float32
    pos = np.arange(S, dtype=f32)
    theta = (1.0 / (f32(ROT_BASE) ** np.linspace(0.0, 1.0, dk // 2, dtype=f32))).astype(f32)
    ang = pos[:, None] * theta[None, :]
    cos = np.repeat(np.cos(ang), 2, axis=1)
    sin = np.repeat(np.sin(ang), 2, axis=1) * np.tile(np.array([-1.0, 1.0], f32), dk // 2)
    log_gamma = np.log(1.0 - 2.0 ** (-5.0 - np.arange(N_HEADS, dtype=f32))).astype(f32)
    i = np.arange(C)
    diff = i[:, None] - i[None, :]
    k_scale = f32(dk ** -0.5)
    decay = np.where(diff >= 0,
                     np.exp(np.maximum(diff, 0).astype(f32) * log_gamma[:, None, None]),
                     0.0) * k_scale
    zeta = np.exp((C - 1 - i).astype(f32)[None, :] * log_gamma[:, None]) * k_scale
    xi = np.exp((i + 1).astype(f32)[None, :] * log_gamma[:, None])
    zeta = np.broadcast_to(zeta[:, :, None], (N_HEADS, C, dk))
    xi = np.broadcast_to(xi[:, :, None], (N_HEADS, C, dk))
    chunk_decay = np.exp(C * log_gamma)
    rope = np.concatenate([cos, sin], axis=1)
    return tuple(jnp.asarray(np.ascontiguousarray(t, dtype=f32))
                 for t in (rope, decay, zeta, xi, chunk_decay))


def _ret_chains(cd_ref, q_ref, k_ref, v_ref, z_ref, cos_ref, sin_ref, dm_ref, zeta_ref, xi_ref,
                o_ref, r_ref, first):
    C = RET_CHUNK
    n_chunks = q_ref.shape[0] // C

    @pl.when(first)
    def _():
        r_ref[...] = jnp.zeros_like(r_ref)

    lane = lax.broadcasted_iota(jnp.int32, (C, HEAD_DIM), 1)
    even = (lane % 2) == 0

    def rot(t, cosv, sinv):
        partner = jnp.where(even, pltpu.roll(t, HEAD_DIM - 1, 1), pltpu.roll(t, 1, 1))
        return t * cosv + partner * sinv

    def head(h):
        cols = slice(h * HEAD_DIM, (h + 1) * HEAD_DIM)
        for c in range(n_chunks):
            rows = slice(c * C, (c + 1) * C)
            cosv = cos_ref[rows, :]
            sinv = sin_ref[rows, :]
            q = rot(q_ref[rows, cols].astype(F32), cosv, sinv)
            k = rot(k_ref[rows, cols].astype(F32), cosv, sinv)
            v = v_ref[rows, cols]
            qb = q.astype(BF16)
            yield
            inner = lax.dot_general(qb, k.astype(BF16), NT_DIMS,
                                    preferred_element_type=F32) * dm_ref[h]
            r_old = r_ref[h]
            kz = (k * zeta_ref[h]).astype(BF16)
            r_ref[h] = r_old * cd_ref[h] + lax.dot_general(kz, v, TN_DIMS,
                                                           preferred_element_type=F32)
            yield
            lhs = jnp.concatenate([inner.astype(BF16), (q * xi_ref[h]).astype(BF16)], axis=1)
            rhs = jnp.concatenate([v, r_old.astype(BF16)], axis=0)
            o = jnp.dot(lhs, rhs, preferred_element_type=F32)
            yield
            o = o * lax.rsqrt(jnp.mean(o * o, axis=-1, keepdims=True) + EPS)
            o_ref[rows, cols] = (o * _silu(z_ref[rows, cols].astype(F32))).astype(o_ref.dtype)
            yield

    return [head(h) for h in range(N_HEADS)]


def _attn_kernel(n_cast, rb_ref, qa_ref, ka_ref, va_ref, za_ref, tab_ref,
                 cd_ref, mixb_ref, rope_ref, dm_ref, zeta_ref, xi_ref, hx_ref, wg_ref, *refs):
    w32_refs = refs[:n_cast]
    ya_ref, yb_ref, gate_ref = refs[n_cast:n_cast + 3]
    w16_refs = refs[n_cast + 3:2 * n_cast + 3]
    vt_ref, s_ref, r_ref = refs[2 * n_cast + 3:]

    def gate_proj():
        tm, tn = gate_ref.shape
        for _ in range(GATE_DELAY):
            yield
        for r0 in range(0, tm, GATE_ROWS):
            for c0 in range(0, tn, GATE_COLS):
                acc = jnp.dot(hx_ref[r0:r0 + GATE_ROWS, :],
                              wg_ref[:, c0:c0 + GATE_COLS].astype(BF16),
                              preferred_element_type=F32)
                gate_ref[r0:r0 + GATE_ROWS, c0:c0 + GATE_COLS] = acc.astype(gate_ref.dtype)
                for _ in range(GATE_EVERY):
                    yield

    def casts():
        for w32, w16 in zip(w32_refs, w16_refs):
            w16[...] = w32[...].astype(BF16)
            yield

    qb_ref, kb_ref, vb_ref, zb_ref = (mixb_ref.at[:, c * D_A:(c + 1) * D_A] for c in range(4))
    cos_ref, sin_ref = (rope_ref.at[:, c * HEAD_DIM:(c + 1) * HEAD_DIM] for c in range(2))
    heads = _ret_chains(cd_ref, qb_ref, kb_ref, vb_ref, zb_ref, cos_ref, sin_ref, dm_ref, zeta_ref,
                        xi_ref, yb_ref, r_ref, first=pl.program_id(1) == 0)
    lanes = [_delayed(_chain(heads[i::RET_LANES]), RET_DELAY) for i in range(RET_LANES)]
    moba = _moba_sweeps(rb_ref, qa_ref, ka_ref, va_ref, za_ref, tab_ref, ya_ref, vt_ref, s_ref)
    _round_robin([moba] + lanes + [casts(), gate_proj()])


def _attention(proj3, tabs, rel_bias, weights, hx, w_in, gate_col0, tm, tn):
    B, S, _ = proj3.shape
    L = MOBA_BLOCK
    C = RET_CHUNK
    dk = HEAD_DIM
    rows_b = S // N_HEADS
    assert sorted(qi for grp in MOBA_GROUPS for qi in grp) == list(range(S // L))
    assert rows_b % C == 0
    n_steps = B * N_HEADS
    gate_tiles = 2 * D_MODEL // tn
    assert (B * S // tm) * gate_tiles == n_steps
    rope, decay, zeta, xi, chunk_decay = _retention_tables(S)

    def gate_tile(b, h):
        step = b * N_HEADS + h
        return step // gate_tiles, step % gate_tiles

    def head_cols(col0):
        return pl.BlockSpec((None, S, HEAD_DIM), lambda b, h: (b, 0, col0 + h))

    def all_heads(col0):
        return pl.BlockSpec((None, rows_b, D_A), lambda b, h: (b, h, col0))

    def w_slice(w):
        rows = max(w.shape[0] // n_steps, BF16_SUBLANES)
        n_slices = w.shape[0] // rows
        assert rows * n_slices == w.shape[0] and n_slices <= n_steps, w.shape
        return pl.BlockSpec((rows, w.shape[1]),
                            lambda b, h: (jnp.minimum(b * N_HEADS + h, n_slices - 1), 0))

    head_tab = pl.BlockSpec((N_HEADS, C, dk), lambda b, h: (0, 0, 0))
    w_specs = [w_slice(w) for w in weights]
    smem = pl.BlockSpec(memory_space=pltpu.SMEM)
    outs = pl.pallas_call(
        functools.partial(_attn_kernel, len(weights)),
        name="attention",
        grid=(B, N_HEADS),
        in_specs=[
            smem,
            head_cols(COL_QA), head_cols(COL_KA), head_cols(COL_VA), head_cols(COL_ZA),
            pl.BlockSpec((None, 2, L, L), lambda b, h: (h, 0, 0, 0)),
            smem,
            pl.BlockSpec((None, rows_b, 4 * D_A), lambda b, h: (b, h, COL_MIXB)),
            pl.BlockSpec((rows_b, 2 * dk), lambda b, h: (h, 0)),
            head_tab, head_tab, head_tab,
            pl.BlockSpec((tm, hx.shape[1]), lambda b, h: (gate_tile(b, h)[0], 0)),
            pl.BlockSpec((w_in.shape[0], tn),
                         lambda b, h: (0, gate_col0 // tn + gate_tile(b, h)[1])),
            *w_specs,
        ],
        out_specs=[head_cols(0), all_heads(0), pl.BlockSpec((tm, tn), gate_tile), *w_specs],
        out_shape=[jax.ShapeDtypeStruct((B, S, D_A), BF16)] * 2
        + [jax.ShapeDtypeStruct((B * S, 2 * D_MODEL), BF16)]
        + [jax.ShapeDtypeStruct(w.shape, BF16) for w in weights],
        scratch_shapes=[
            pltpu.VMEM((HEAD_DIM + VT_PAD, S), BF16),
            pltpu.VMEM((2 * len(MOBA_GROUPS[0]), S, L), F32),
            pltpu.VMEM((N_HEADS, dk, dk), F32),
        ],
        compiler_params=pltpu.CompilerParams(
            dimension_semantics=("arbitrary", "arbitrary"),
            vmem_limit_bytes=VMEM_LIMIT_V7X),
    )(rel_bias, proj3, proj3, proj3, proj3, tabs,
      chunk_decay, proj3, rope, decay, zeta, xi, hx, w_in, *weights)
    return outs[0], outs[1], outs[2], outs[3:]


def _tail_kernel(ya_ref, yb_ref, ga_ref, gb_ref, x_ref, p_ref, wa_ref, wb_ref, wo_ref, wpg_hbm,
                 wpp_ref, gple_ref, gfin_ref, o_ref, x1_ref, pe_ref, wpg_ref, wpg_sem):
    s = pl.program_id(0)
    last = pl.num_programs(0) - 1

    def first_half():
        ua = jnp.dot(ya_ref[...], wa_ref[...], preferred_element_type=F32)
        yield
        ub = jnp.dot(yb_ref[...], wb_ref[...], preferred_element_type=F32)
        merged = (jax.nn.sigmoid(ga_ref[...].astype(F32)) * ua
                  + jax.nn.sigmoid(gb_ref[...].astype(F32)) * ub)
        yield
        x1 = x_ref[...] + jnp.dot(merged.astype(BF16), wo_ref[...], preferred_element_type=F32)
        pe = jnp.dot(p_ref[...].astype(BF16), wpp_ref[...], preferred_element_type=F32)
        yield
        x1_ref[...] = x1
        pe_ref[...] = pe

    def second_half(x1, pe):
        hp = _rms_norm(x1, gple_ref[...]).astype(BF16)
        yield
        gate = jax.nn.sigmoid(jnp.dot(hp, wpg_ref[...], preferred_element_type=F32))
        x2 = x1 + gate * pe
        yield
        o_ref[...] = _rms_norm(x2, gfin_ref[...])

    wpg_copy = pltpu.make_async_copy(wpg_hbm, wpg_ref, wpg_sem)

    @pl.when(s == 0)
    def _():
        wpg_copy.start()
        _round_robin([first_half()])

    @pl.when(s == 1)
    def _():
        wpg_copy.wait()

    @pl.when((s > 0) & (s < last))
    def _():
        _round_robin([second_half(x1_ref[...], pe_ref[...]), first_half()])

    @pl.when(s == last)
    def _():
        _round_robin([second_half(x1_ref[...], pe_ref[...])])


def _tail(ya, yb, gates, x2d, p2d, wa, wb, wo, wpg, wpp, g_ple, g_final, tm):
    m, d = x2d.shape
    n_t = m // tm

    def const(shape):
        return pl.BlockSpec(shape, lambda s: (0, 0), pipeline_mode=pl.Buffered(1))

    def first(s):
        return jnp.minimum(s, n_t - 1)

    def second(s):
        return jnp.maximum(s - 1, 0)

    return pl.pallas_call(
        _tail_kernel,
        name="tail",
        grid=(n_t + 1,),
        in_specs=[
            pl.BlockSpec((tm, D_A), lambda s: (first(s), 0)),
            pl.BlockSpec((tm, D_A), lambda s: (first(s), 0)),
            pl.BlockSpec((tm, d), lambda s: (first(s), 0)),
            pl.BlockSpec((tm, d), lambda s: (first(s), 1)),
            pl.BlockSpec((tm, d), lambda s: (first(s), 0)),
            pl.BlockSpec((tm, D_PLE), lambda s: (first(s), 0)),
            const((D_A, d)), const((D_A, d)), const((d, d)),
            pl.BlockSpec(memory_space=pl.ANY),
            const((D_PLE, d)),
            const((1, d)), const((1, d)),
        ],
        out_specs=pl.BlockSpec((tm, d), lambda s: (second(s), 0)),
        out_shape=jax.ShapeDtypeStruct((m, d), F32),
        scratch_shapes=[pltpu.VMEM((tm, d), F32), pltpu.VMEM((tm, d), F32),
                        pltpu.VMEM((d, d), BF16), pltpu.SemaphoreType.DMA(())],
        compiler_params=pltpu.CompilerParams(
            dimension_semantics=("arbitrary",),
            vmem_limit_bytes=VMEM_LIMIT_V7X),
    )(ya, yb, gates, gates, x2d, p2d, wa, wb, wo, wpg, wpp, g_ple, g_final)


def kernel(x, p, g_mix, w_in, w_a, w_b, w_out, g_ple, w_ple_gate, w_ple_proj, rel_bias, g_final):
    B, S, d = x.shape
    assert w_in.shape[0] == 1, "the fused tail applies the final norm: single-layer stacks only"
    tabs = _bias_tables(rel_bias)
    x2d = x.reshape(B * S, d)
    n_mix = 8 * D_A
    col_scale = jnp.asarray(
        np.where(np.arange(n_mix) < D_A, HEAD_DIM ** -0.5 * LOG2E, 1.0).astype(np.float32))
    proj, hx = _project(x2d, g_mix, w_in[0], col_scale[None, :], n_mix, PROJ_TM, PROJ_TN)
    proj3 = proj.reshape(B, S, n_mix)
    ya, yb, gates, (wa, wb, wo, wpg, wpp) = _attention(
        proj3, tabs, rel_bias, (w_a[0], w_b[0], w_out[0], w_ple_gate[0], w_ple_proj[0]),
        hx, w_in[0], n_mix, PROJ_TM, PROJ_TN)
    out = _tail(ya.reshape(B * S, D_A), yb.reshape(B * S, D_A), gates, x2d,
                p[0].reshape(B * S, D_PLE), wa, wb, wo, wpg, wpp,
                g_ple, g_final[None, :], TAIL_TM)
    return out.reshape(B, S, d)
```

```python
import functools
import math

import numpy as np
import jax
import jax.numpy as jnp
from jax import lax
from jax.experimental import pallas as pl
from jax.experimental.pallas import tpu as pltpu

D_MODEL = 2048
N_HEADS = 8
HEAD_DIM = 128
D_A = N_HEADS * HEAD_DIM
MOBA_BLOCK = 256
MOBA_TOPK = 3
N_BUCKETS = 32
MAX_DISTANCE = 128
RET_CHUNK = 128
ROT_BASE = 10000.0
D_PLE = 256
EPS = 1e-6

COL_QA, COL_KA, COL_VA, COL_ZA = (i * N_HEADS for i in range(4))
COL_MIXB = 1

VMEM_LIMIT_V7X = 56 * 1024 * 1024
PROJ_TM = 1024
PROJ_TN = 1024
TAIL_TM = 256
BF16_SUBLANES = 16
VT_PAD = BF16_SUBLANES
MOBA_GROUPS = ((7, 0), (6, 1), (5, 2), (4, 3))
RET_LANES = 2
RET_DELAY = 5
X_PARTS = 4
GATE_ROWS = 1024
GATE_COLS = 256
GATE_EVERY = 8
GATE_DELAY = 2

F32 = jnp.float32
BF16 = jnp.bfloat16
NEG_INF = float("-inf")
LOG2E = math.log2(math.e)
NT_DIMS = (((1,), (1,)), ((), ()))
TN_DIMS = (((0,), (0,)), ((), ()))


def _rms_norm(x, g):
    return x * lax.rsqrt(jnp.mean(x * x, axis=-1, keepdims=True) + EPS) * g


def _silu(z):
    return z * jax.nn.sigmoid(z)


def _sweeps(gens):
    gens = list(gens)
    while gens:
        for g in list(gens):
            try:
                next(g)
            except StopIteration:
                gens.remove(g)
        yield


def _round_robin(gens):
    for _ in _sweeps(gens):
        pass


def _chain(gens):
    for g in gens:
        yield from g


def _delayed(gen, sweeps):
    for _ in range(sweeps):
        yield
    yield from gen


def _proj_kernel(*refs):
    x_refs, (g_ref, w_ref, cs_ref, o_ref, h_ref) = refs[:X_PARTS], refs[X_PARTS:]

    rows = h_ref.shape[0] // X_PARTS

    @pl.when(pl.program_id(1) == 0)
    def _():
        w = w_ref[...].astype(BF16)
        for q, x_ref in enumerate(x_refs):
            part = slice(q * rows, (q + 1) * rows)
            hq = _rms_norm(x_ref[...], g_ref[...]).astype(BF16)
            h_ref[part, :] = hq
            acc = jnp.dot(hq, w, preferred_element_type=F32)
            o_ref[part, :] = (acc * cs_ref[...]).astype(o_ref.dtype)

    @pl.when(pl.program_id(1) != 0)
    def _():
        acc = jnp.dot(h_ref[...], w_ref[...].astype(BF16), preferred_element_type=F32)
        o_ref[...] = (acc * cs_ref[...]).astype(o_ref.dtype)


def _project(x2d, g, w, col_scale, n, tm, tn):
    m, d = x2d.shape
    n_i, n_j = m // tm, n // tn
    assert n_j > X_PARTS, "every x slice must still hold this row tile at column step 0"

    def x_part(q):
        def index(i, j):
            nxt = jnp.minimum(i + (j >= n_j - X_PARTS + q).astype(jnp.int32), n_i - 1)
            return (nxt * X_PARTS + q, 0)
        return pl.BlockSpec((tm // X_PARTS, d), index)

    return pl.pallas_call(
        _proj_kernel,
        name="proj",
        grid=(n_i, n_j),
        in_specs=[
            *[x_part(q) for q in range(X_PARTS)],
            pl.BlockSpec((1, d), lambda i, j: (0, 0)),
            pl.BlockSpec((d, tn), lambda i, j: (0, j)),
            pl.BlockSpec((1, tn), lambda i, j: (0, j)),
        ],
        out_specs=[pl.BlockSpec((tm, tn), lambda i, j: (i, j)),
                   pl.BlockSpec((tm, d), lambda i, j: (i, 0))],
        out_shape=[jax.ShapeDtypeStruct((m, n), BF16), jax.ShapeDtypeStruct((m, d), BF16)],
        compiler_params=pltpu.CompilerParams(
            dimension_semantics=("arbitrary", "arbitrary"),
            vmem_limit_bytes=VMEM_LIMIT_V7X),
    )(*[x2d] * X_PARTS, g, w, col_scale)


def _t5_bucket_np(n):
    max_exact = N_BUCKETS // 2
    nf = np.maximum(n, 1).astype(np.float32)
    val = (np.log(nf / max_exact) / np.float32(math.log(MAX_DISTANCE / max_exact))
           * (N_BUCKETS - max_exact))
    large = np.minimum(max_exact + val.astype(np.int32), N_BUCKETS - 1)
    return np.where(n < max_exact, n, large).astype(np.int32)


def _bucket_row():
    return _t5_bucket_np(np.arange(2 * MOBA_BLOCK))[None, :].astype(np.int32)


def _bias_kernel(rb_ref, bkt_ref, o_ref):
    L = MOBA_BLOCK
    bkt = bkt_ref[...]
    key = lax.broadcasted_iota(jnp.int32, (L, L), 0)
    qry = lax.broadcasted_iota(jnp.int32, (L, L), 1)
    for h in range(N_HEADS):
        f = jnp.zeros(bkt.shape, F32)
        for b in range(N_BUCKETS):
            f = jnp.where(bkt == b, rb_ref[b, h] * LOG2E, f)
        both = pltpu.roll(jnp.broadcast_to(f, (L, 2 * L)), 0, 1, stride=1, stride_axis=0)
        o_ref[h, 0] = jnp.where(qry >= key, both[:, :L], NEG_INF)
        o_ref[h, 1] = both[:, L:]


def _bias_tables(rel_bias):
    bkt = jnp.asarray(_bucket_row())
    L = MOBA_BLOCK
    return pl.pallas_call(
        _bias_kernel,
        name="t5_bias",
        in_specs=[
            pl.BlockSpec(memory_space=pltpu.SMEM),
            pl.BlockSpec(memory_space=pltpu.VMEM),
        ],
        out_specs=pl.BlockSpec(memory_space=pltpu.VMEM),
        out_shape=jax.ShapeDtypeStruct((N_HEADS, 2, L, L), F32),
    )(rel_bias, bkt)


def _moba_sweeps(rb_ref, q_ref, k_ref, v_ref, z_ref, tab_ref, o_ref, vt_ref, s_ref):
    L = MOBA_BLOCK
    nb = k_ref.shape[0] // L
    h = pl.program_id(1)

    kf = k_ref[...].astype(F32).reshape(nb, L, HEAD_DIM)
    km = jnp.mean(kf, axis=1)
    km_hi = km.astype(BF16)
    km_lo = (km - km_hi.astype(F32)).astype(BF16)
    km2 = jnp.concatenate([km_hi, km_lo], axis=0)
    vt_ref[:HEAD_DIM, :] = v_ref[...].T
    ones_row = lax.broadcasted_iota(jnp.int32, (VT_PAD, vt_ref.shape[1]), 0) == 0
    vt_ref[HEAD_DIM:, :] = jnp.where(ones_row, 1.0, 0.0).astype(BF16)
    far_bias = rb_ref[N_BUCKETS - 1, h] * LOG2E
    row = lax.broadcasted_iota(jnp.int32, (nb, L), 0)

    def scores(qi, slot, out):
        q = q_ref[qi * L:(qi + 1) * L, :]
        sel_add = None
        if qi > MOBA_TOPK:
            g2 = lax.dot_general(km2, q, NT_DIMS, preferred_element_type=F32)
            g = g2[:nb] + g2[nb:]
            cnt = jnp.zeros((nb, L), jnp.int32)
            for mm in range(qi):
                gm = g[mm:mm + 1, :]
                ge = jnp.where(gm >= g, 1, 0)
                gt = jnp.where(gm > g, 1, 0)
                cnt = cnt + jnp.where(row > mm, ge, gt)
            sel_add = jnp.where(cnt < MOBA_TOPK, 0.0, NEG_INF)

        sbuf = s_ref.at[slot]
        shifts = []
        m = None
        for n in range(qi + 1):
            s = lax.dot_general(k_ref[n * L:(n + 1) * L, :], q, NT_DIMS,
                                preferred_element_type=F32)
            shift = None
            if n == qi:
                s = s + tab_ref[0]
            else:
                if n == qi - 1:
                    s = s + tab_ref[1]
                if sel_add is not None:
                    shift = sel_add[n:n + 1, :]
                if n < qi - 1:
                    shift = far_bias if shift is None else shift + far_bias
            sbuf[n * L:(n + 1) * L, :] = s
            tmax = jnp.max(s, axis=0, keepdims=True)
            if shift is not None:
                tmax = tmax + shift
            shifts.append(shift)
            m = tmax if m is None else jnp.maximum(m, tmax)
            yield
        out.append((m, shifts))

    def outputs(qi, slot, m, shifts):
        sbuf = s_ref.at[slot]
        acc = None
        for n in range(qi + 1):
            off = m if shifts[n] is None else m - shifts[n]
            p = jnp.exp2((sbuf[n * L:(n + 1) * L, :] - off).astype(BF16))
            pv = jnp.dot(vt_ref[:, n * L:(n + 1) * L], p,
                         preferred_element_type=F32)
            acc = pv if acc is None else acc + pv
            yield
        l = acc[HEAD_DIM:HEAD_DIM + 1, :]
        y = (acc[:HEAD_DIM, :] * (1.0 / l)).T
        z = z_ref[qi * L:(qi + 1) * L, :].astype(F32)
        o_ref[qi * L:(qi + 1) * L, :] = (y * _silu(z)).astype(o_ref.dtype)

    width = len(MOBA_GROUPS[0])
    pending = []
    for t, grp in enumerate(MOBA_GROUPS):
        outs = [[] for _ in grp]
        slots = [(t % 2) * width + i for i in range(width)]
        yield from _sweeps([scores(qi, slots[i], outs[i]) for i, qi in enumerate(grp)]
                           + [outputs(*args) for args in pending])
        pending = [(qi, slots[i]) + outs[i][0] for i, qi in enumerate(grp)]
    yield from _sweeps([outputs(*args) for args in pending])


def _retention_tables(S):
    C = RET_CHUNK
    dk = HEAD_DIM
    f32 = np.float32
    pos = np.arange(S, dtype=f32)
    theta = (1.0 / (f32(ROT_BASE) ** np.linspace(0.0, 1.0, dk // 2, dtype=f32))).astype(f32)
    ang = pos[:, None] * theta[None, :]
    cos = np.repeat(np.cos(ang), 2, axis=1)
    sin = np.repeat(np.sin(ang), 2, axis=1) * np.tile(np.array([-1.0, 1.0], f32), dk // 2)
    log_gamma = np.log(1.0 - 2.0 ** (-5.0 - np.arange(N_HEADS, dtype=f32))).astype(f32)
    i = np.arange(C)
    diff = i[:, None] - i[None, :]
    k_scale = f32(dk ** -0.5)
    decay = np.where(diff >= 0,
                     np.exp(np.maximum(diff, 0).astype(f32) * log_gamma[:, None, None]),
                     0.0) * k_scale
    zeta = np.exp((C - 1 - i).astype(f32)[None, :] * log_gamma[:, None]) * k_scale
    xi = np.exp((i + 1).astype(f32)[None, :] * log_gamma[:, None])
    zeta = np.broadcast_to(zeta[:, :, None], (N_HEADS, C, dk))
    xi = np.broadcast_to(xi[:, :, None], (N_HEADS, C, dk))
    chunk_decay = np.exp(C * log_gamma)
    rope = np.concatenate([cos, sin], axis=1)
    return tuple(jnp.asarray(np.ascontiguousarray(t, dtype=f32))
                 for t in (rope, decay, zeta, xi, chunk_decay))


def _ret_chains(cd_ref, q_ref, k_ref, v_ref, z_ref, cos_ref, sin_ref, dm_ref, zeta_ref, xi_ref,
                o_ref, r_ref, first):
    C = RET_CHUNK
    n_chunks = q_ref.shape[0] // C

    @pl.when(first)
    def _():
        r_ref[...] = jnp.zeros_like(r_ref)

    lane = lax.broadcasted_iota(jnp.int32, (C, HEAD_DIM), 1)
    even = (lane % 2) == 0

    def rot(t, cosv, sinv):
        partner = jnp.where(even, pltpu.roll(t, HEAD_DIM - 1, 1), pltpu.roll(t, 1, 1))
        return t * cosv + partner * sinv

    def head(h):
        cols = slice(h * HEAD_DIM, (h + 1) * HEAD_DIM)
        for c in range(n_chunks):
            rows = slice(c * C, (c + 1) * C)
            cosv = cos_ref[rows, :]
            sinv = sin_ref[rows, :]
            q = rot(q_ref[rows, cols].astype(F32), cosv, sinv)
            k = rot(k_ref[rows, cols].astype(F32), cosv, sinv)
            v = v_ref[rows, cols]
            qb = q.astype(BF16)
            yield
            inner = lax.dot_general(qb, k.astype(BF16), NT_DIMS,
                                    preferred_element_type=F32) * dm_ref[h]
            r_old = r_ref[h]
            kz = (k * zeta_ref[h]).astype(BF16)
            r_ref[h] = r_old * cd_ref[h] + lax.dot_general(kz, v, TN_DIMS,
                                                           preferred_element_type=F32)
            yield
            lhs = jnp.concatenate([inner.astype(BF16), (q * xi_ref[h]).astype(BF16)], axis=1)
            rhs = jnp.concatenate([v, r_old.astype(BF16)], axis=0)
            o = jnp.dot(lhs, rhs, preferred_element_type=F32)
            yield
            o = o * lax.rsqrt(jnp.mean(o * o, axis=-1, keepdims=True) + EPS)
            o_ref[rows, cols] = (o * _silu(z_ref[rows, cols].astype(F32))).astype(o_ref.dtype)
            yield

    return [head(h) for h in range(N_HEADS)]


def _attn_kernel(n_cast, rb_ref, qa_ref, ka_ref, va_ref, za_ref, tab_ref,
                 cd_ref, mixb_ref, rope_ref, dm_ref, zeta_ref, xi_ref, hx_ref, wg_ref, *refs):
    w32_refs = refs[:n_cast]
    ya_ref, yb_ref, gate_ref = refs[n_cast:n_cast + 3]
    w16_refs = refs[n_cast + 3:2 * n_cast + 3]
    vt_ref, s_ref, r_ref = refs[2 * n_cast + 3:]

    def gate_proj():
        tm, tn = gate_ref.shape
        for _ in range(GATE_DELAY):
            yield
        for r0 in range(0, tm, GATE_ROWS):
            for c0 in range(0, tn, GATE_COLS):
                acc = jnp.dot(hx_ref[r0:r0 + GATE_ROWS, :],
                              wg_ref[:, c0:c0 + GATE_COLS].astype(BF16),
                              preferred_element_type=F32)
                gate_ref[r0:r0 + GATE_ROWS, c0:c0 + GATE_COLS] = acc.astype(gate_ref.dtype)
                for _ in range(GATE_EVERY):
                    yield

    def casts():
        for w32, w16 in zip(w32_refs, w16_refs):
            w16[...] = w32[...].astype(BF16)
            yield

    qb_ref, kb_ref, vb_ref, zb_ref = (mixb_ref.at[:, c * D_A:(c + 1) * D_A] for c in range(4))
    cos_ref, sin_ref = (rope_ref.at[:, c * HEAD_DIM:(c + 1) * HEAD_DIM] for c in range(2))
    heads = _ret_chains(cd_ref, qb_ref, kb_ref, vb_ref, zb_ref, cos_ref, sin_ref, dm_ref, zeta_ref,
                        xi_ref, yb_ref, r_ref, first=pl.program_id(1) == 0)
    lanes = [_delayed(_chain(heads[i::RET_LANES]), RET_DELAY) for i in range(RET_LANES)]
    moba = _moba_sweeps(rb_ref, qa_ref, ka_ref, va_ref, za_ref, tab_ref, ya_ref, vt_ref, s_ref)
    _round_robin([moba] + lanes + [casts(), gate_proj()])


def _attention(proj3, tabs, rel_bias, weights, hx, w_in, gate_col0, tm, tn):
    B, S, _ = proj3.shape
    L = MOBA_BLOCK
    C = RET_CHUNK
    dk = HEAD_DIM
    rows_b = S // N_HEADS
    assert sorted(qi for grp in MOBA_GROUPS for qi in grp) == list(range(S // L))
    assert rows_b % C == 0
    n_steps = B * N_HEADS
    gate_tiles = 2 * D_MODEL // tn
    assert (B * S // tm) * gate_tiles == n_steps
    rope, decay, zeta, xi, chunk_decay = _retention_tables(S)

    def gate_tile(b, h):
        step = b * N_HEADS + h
        return step // gate_tiles, step % gate_tiles

    def head_cols(col0):
        return pl.BlockSpec((None, S, HEAD_DIM), lambda b, h: (b, 0, col0 + h))

    def all_heads(col0):
        return pl.BlockSpec((None, rows_b, D_A), lambda b, h: (b, h, col0))

    def w_slice(w):
        rows = max(w.shape[0] // n_steps, BF16_SUBLANES)
        n_slices = w.shape[0] // rows
        assert rows * n_slices == w.shape[0] and n_slices <= n_steps, w.shape
        return pl.BlockSpec((rows, w.shape[1]),
                            lambda b, h: (jnp.minimum(b * N_HEADS + h, n_slices - 1), 0))

    head_tab = pl.BlockSpec((N_HEADS, C, dk), lambda b, h: (0, 0, 0))
    w_specs = [w_slice(w) for w in weights]
    smem = pl.BlockSpec(memory_space=pltpu.SMEM)
    outs = pl.pallas_call(
        functools.partial(_attn_kernel, len(weights)),
        name="attention",
        grid=(B, N_HEADS),
        in_specs=[
            smem,
            head_cols(COL_QA), head_cols(COL_KA), head_cols(COL_VA), head_cols(COL_ZA),
            pl.BlockSpec((None, 2, L, L), lambda b, h: (h, 0, 0, 0)),
            smem,
            pl.BlockSpec((None, rows_b, 4 * D_A), lambda b, h: (b, h, COL_MIXB)),
            pl.BlockSpec((rows_b, 2 * dk), lambda b, h: (h, 0)),
            head_tab, head_tab, head_tab,
            pl.BlockSpec((tm, hx.shape[1]), lambda b, h: (gate_tile(b, h)[0], 0)),
            pl.BlockSpec((w_in.shape[0], tn),
                         lambda b, h: (0, gate_col0 // tn + gate_tile(b, h)[1])),
            *w_specs,
        ],
        out_specs=[head_cols(0), all_heads(0), pl.BlockSpec((tm, tn), gate_tile), *w_specs],
        out_shape=[jax.ShapeDtypeStruct((B, S, D_A), BF16)] * 2
        + [jax.ShapeDtypeStruct((B * S, 2 * D_MODEL), BF16)]
        + [jax.ShapeDtypeStruct(w.shape, BF16) for w in weights],
        scratch_shapes=[
            pltpu.VMEM((HEAD_DIM + VT_PAD, S), BF16),
            pltpu.VMEM((2 * len(MOBA_GROUPS[0]), S, L), F32),
            pltpu.VMEM((N_HEADS, dk, dk), F32),
        ],
        compiler_params=pltpu.CompilerParams(
            dimension_semantics=("arbitrary", "arbitrary"),
            vmem_limit_bytes=VMEM_LIMIT_V7X),
    )(rel_bias, proj3, proj3, proj3, proj3, tabs,
      chunk_decay, proj3, rope, decay, zeta, xi, hx, w_in, *weights)
    return outs[0], outs[1], outs[2], outs[3:]


def _tail_kernel(ya_ref, yb_ref, ga_ref, gb_ref, x_ref, p_ref, wa_ref, wb_hbm, wo_hbm, wpg_hbm,
                 wpp_hbm, gple_ref, gfin_ref, o_ref, x1_ref, pe_ref,
                 wb_ref, wo_ref, wpg_ref, wpp_ref, sems):
    s = pl.program_id(0)
    last = pl.num_programs(0) - 1

    copies = {name: pltpu.make_async_copy(src, dst, sems.at[i]) for i, (name, src, dst) in
              enumerate((("wb", wb_hbm, wb_ref), ("wo", wo_hbm, wo_ref),
                         ("wpg", wpg_hbm, wpg_ref), ("wpp", wpp_hbm, wpp_ref)))}

    def first_half(wait=lambda *names: None):
        ua = jnp.dot(ya_ref[...], wa_ref[...], preferred_element_type=F32)
        yield
        wait("wb")
        ub = jnp.dot(yb_ref[...], wb_ref[...], preferred_element_type=F32)
        merged = (jax.nn.sigmoid(ga_ref[...].astype(F32)) * ua
                  + jax.nn.sigmoid(gb_ref[...].astype(F32)) * ub)
        yield
        wait("wo", "wpp")
        x1 = x_ref[...] + jnp.dot(merged.astype(BF16), wo_ref[...], preferred_element_type=F32)
        pe = jnp.dot(p_ref[...].astype(BF16), wpp_ref[...], preferred_element_type=F32)
        yield
        x1_ref[...] = x1
        pe_ref[...] = pe

    def second_half(x1, pe):
        hp = _rms_norm(x1, gple_ref[...]).astype(BF16)
        yield
        gate = jax.nn.sigmoid(jnp.dot(hp, wpg_ref[...], preferred_element_type=F32))
        x2 = x1 + gate * pe
        yield
        o_ref[...] = _rms_norm(x2, gfin_ref[...])

    @pl.when(s == 0)
    def _():
        for name in ("wb", "wo", "wpp", "wpg"):
            copies[name].start()
        _round_robin([first_half(lambda *names: [copies[n].wait() for n in names])])

    @pl.when(s == 1)
    def _():
        copies["wpg"].wait()

    @pl.when((s > 0) & (s < last))
    def _():
        _round_robin([second_half(x1_ref[...], pe_ref[...]), first_half()])

    @pl.when(s == last)
    def _():
        _round_robin([second_half(x1_ref[...], pe_ref[...])])


def _tail(ya, yb, gates, x2d, p2d, wa, wb, wo, wpg, wpp, g_ple, g_final, tm):
    m, d = x2d.shape
    n_t = m // tm

    def const(shape):
        return pl.BlockSpec(shape, lambda s: (0, 0), pipeline_mode=pl.Buffered(1))

    def first(s):
        return jnp.minimum(s, n_t - 1)

    def second(s):
        return jnp.maximum(s - 1, 0)

    return pl.pallas_call(
        _tail_kernel,
        name="tail",
        grid=(n_t + 1,),
        in_specs=[
            pl.BlockSpec((tm, D_A), lambda s: (first(s), 0)),
            pl.BlockSpec((tm, D_A), lambda s: (first(s), 0)),
            pl.BlockSpec((tm, d), lambda s: (first(s), 0)),
            pl.BlockSpec((tm, d), lambda s: (first(s), 1)),
            pl.BlockSpec((tm, d), lambda s: (first(s), 0)),
            pl.BlockSpec((tm, D_PLE), lambda s: (first(s), 0)),
            const((D_A, d)),
            *[pl.BlockSpec(memory_space=pl.ANY)] * 4,

            const((1, d)), const((1, d)),
        ],
        out_specs=pl.BlockSpec((tm, d), lambda s: (second(s), 0)),
        out_shape=jax.ShapeDtypeStruct((m, d), F32),
        scratch_shapes=[pltpu.VMEM((tm, d), F32), pltpu.VMEM((tm, d), F32),
                        pltpu.VMEM((D_A, d), BF16), pltpu.VMEM((d, d), BF16),
                        pltpu.VMEM((d, d), BF16), pltpu.VMEM((D_PLE, d), BF16),
                        pltpu.SemaphoreType.DMA((4,))],
        compiler_params=pltpu.CompilerParams(
            dimension_semantics=("arbitrary",),
            vmem_limit_bytes=VMEM_LIMIT_V7X),
    )(ya, yb, gates, gates, x2d, p2d, wa, wb, wo, wpg, wpp, g_ple, g_final)


def kernel(x, p, g_mix, w_in, w_a, w_b, w_out, g_ple, w_ple_gate, w_ple_proj, rel_bias, g_final):
    B, S, d = x.shape
    assert w_in.shape[0] == 1, "the fused tail applies the final norm: single-layer stacks only"
    tabs = _bias_tables(rel_bias)
    x2d = x.reshape(B * S, d)
    n_mix = 8 * D_A
    col_scale = jnp.asarray(
        np.where(np.arange(n_mix) < D_A, HEAD_DIM ** -0.5 * LOG2E, 1.0).astype(np.float32))
    proj, hx = _project(x2d, g_mix, w_in[0], col_scale[None, :], n_mix, PROJ_TM, PROJ_TN)
    proj3 = proj.reshape(B, S, n_mix)
    ya, yb, gates, (wa, wb, wo, wpg, wpp) = _attention(
        proj3, tabs, rel_bias, (w_a[0], w_b[0], w_out[0], w_ple_gate[0], w_ple_proj[0]),
        hx, w_in[0], n_mix, PROJ_TM, PROJ_TN)
    out = _tail(ya.reshape(B * S, D_A), yb.reshape(B * S, D_A), gates, x2d,
                p[0].reshape(B * S, D_PLE), wa, wb, wo, wpg, wpp,
                g_ple, g_final[None, :], TAIL_TM)
    return out.reshape(B, S, d)
```
